```python
import jax, jax.numpy as jnp
from jax import lax
import numpy as np

D_MODEL = 1024
BATCH = 16
SEQ = 2048
DEPTH = 1

CHUNK = 64
FOX_HEADS = 16
HEAD_DIM = 64
FOX_WIDTH = FOX_HEADS * HEAD_DIM
CONV_GROUPS = 16
CONV_WIDTH = D_MODEL
D_MIX = FOX_WIDTH + CONV_WIDTH
CONV_KERNEL = 31
Q_BLOCK = 128
EPS = 1e-6
NEG_INF = -1e30
IN_COLS = 3 * FOX_WIDTH + FOX_HEADS + FOX_WIDTH + 2 * CONV_WIDTH + CONV_WIDTH

kernel_name = "hybrid_fox_conformer_block"


def rmsnorm(x, g):
    xf = x.astype(jnp.float32)
    y = xf * lax.rsqrt(jnp.mean(xf * xf, axis=-1, keepdims=True) + EPS)
    return (y * g.astype(jnp.float32)).astype(x.dtype)


def layernorm(x, g, b):
    xf = x.astype(jnp.float32)
    mu = jnp.mean(xf, axis=-1, keepdims=True)
    var = jnp.mean(jnp.square(xf - mu), axis=-1, keepdims=True)
    y = (xf - mu) * lax.rsqrt(var + EPS)
    return (y * g.astype(jnp.float32) + b.astype(jnp.float32)).astype(x.dtype)


def fox_attention(q, k, v, log_f):
    S = q.shape[1]
    c = jnp.cumsum(log_f, axis=1)
    c = jnp.transpose(c, (0, 2, 1))
    scale = HEAD_DIM ** -0.5
    outs = []
    for i in range(S // Q_BLOCK):
        q0, q1 = i * Q_BLOCK, (i + 1) * Q_BLOCK
        qb = q[:, q0:q1].astype(jnp.float32)
        kb = k[:, :q1].astype(jnp.float32)
        vb = v[:, :q1]
        logits = jnp.einsum('bqhd,bkhd->bhqk', qb, kb) * scale
        decay = c[:, :, q0:q1, None] - c[:, :, None, :q1]
        qpos = jnp.arange(q0, q1)[:, None]
        kpos = jnp.arange(q1)[None, :]
        logits = jnp.where(kpos <= qpos, logits + decay, NEG_INF)
        probs = jax.nn.softmax(logits, axis=-1).astype(v.dtype)
        outs.append(jnp.einsum('bhqk,bkhd->bqhd', probs, vb))
    return jnp.concatenate(outs, axis=1)


def causal_depthwise_conv(u, w, b):
    C = u.shape[-1]
    y = lax.conv_general_dilated(
        u, w.reshape(CONV_KERNEL, 1, C).astype(u.dtype),
        window_strides=(1,), padding=[(CONV_KERNEL - 1, 0)],
        dimension_numbers=('NWC', 'WIO', 'NWC'), feature_group_count=C)
    return y + b.astype(u.dtype)


def setup_inputs(seed: int = 0) -> dict:
    key = jax.random.key(seed)
    ks = jax.random.split(key, 12)
    f32 = jnp.float32
    x = jax.random.normal(ks[0], (BATCH, SEQ, D_MODEL), f32)
    norm_g = 1.0 + 0.02 * jax.random.normal(ks[1], (DEPTH, D_MODEL), f32)
    w_in = jax.random.normal(ks[2], (DEPTH, D_MODEL, IN_COLS), f32) * D_MODEL ** -0.5
    b_forget = (jnp.linspace(1.0, 5.0, FOX_HEADS, dtype=f32)[None, :]
                + 0.1 * jax.random.normal(ks[3], (DEPTH, FOX_HEADS), f32))
    q_norm_g = 1.0 + 0.02 * jax.random.normal(ks[4], (DEPTH, FOX_HEADS, HEAD_DIM), f32)
    k_norm_g = 1.0 + 0.02 * jax.random.normal(ks[5], (DEPTH, FOX_HEADS, HEAD_DIM), f32)
    conv_w = jax.random.normal(ks[6], (DEPTH, CONV_KERNEL, CONV_WIDTH), f32) * CONV_KERNEL ** -0.5
    conv_b = 0.02 * jax.random.normal(ks[7], (DEPTH, CONV_WIDTH), f32)
    conv_ln_g = 1.0 + 0.02 * jax.random.normal(ks[8], (DEPTH, CONV_WIDTH), f32)
    conv_ln_b = 0.02 * jax.random.normal(ks[9], (DEPTH, CONV_WIDTH), f32)
    w_out = jax.random.normal(ks[10], (DEPTH, D_MIX, D_MODEL), f32) * D_MIX ** -0.5
    return {"x": x, "norm_g": norm_g, "w_in": w_in, "b_forget": b_forget,
            "q_norm_g": q_norm_g, "k_norm_g": k_norm_g, "conv_w": conv_w,
            "conv_b": conv_b, "conv_ln_g": conv_ln_g, "conv_ln_b": conv_ln_b,
            "w_out": w_out}


def reference(x, norm_g, w_in, b_forget, q_norm_g, k_norm_g, conv_w, conv_b,
              conv_ln_g, conv_ln_b, w_out):
    B, S, _ = x.shape
    o_q = 0
    o_k = o_q + FOX_WIDTH
    o_v = o_k + FOX_WIDTH
    o_f = o_v + FOX_WIDTH
    o_gf = o_f + FOX_HEADS
    o_glu = o_gf + FOX_WIDTH
    o_gc = o_glu + 2 * CONV_WIDTH
    for l in range(DEPTH):
        h = rmsnorm(x, norm_g[l])
        z = jnp.einsum('bsd,de->bse', h, w_in[l])

        q = z[..., o_q:o_k].reshape(B, S, FOX_HEADS, HEAD_DIM)
        k = z[..., o_k:o_v].reshape(B, S, FOX_HEADS, HEAD_DIM)
        v = z[..., o_v:o_f].reshape(B, S, FOX_HEADS, HEAD_DIM)
        q = rmsnorm(q, q_norm_g[l])
        k = rmsnorm(k, k_norm_g[l])
        log_f = jax.nn.log_sigmoid(z[..., o_f:o_gf].astype(jnp.float32)
                                   + b_forget[l].astype(jnp.float32))
        a = fox_attention(q, k, v, log_f).reshape(B, S, FOX_WIDTH)
        a = a * jax.nn.silu(z[..., o_gf:o_glu])

        u = z[..., o_glu:o_gc]
        u = u[..., :CONV_WIDTH] * jax.nn.sigmoid(u[..., CONV_WIDTH:])
        u = causal_depthwise_conv(u, conv_w[l], conv_b[l])
        u = jax.nn.silu(layernorm(u, conv_ln_g[l], conv_ln_b[l]))
        u = u * jax.nn.silu(z[..., o_gc:])

        y = jnp.concatenate([a, u], axis=-1)
        x = x + jnp.einsum('bse,ed->bsd', y, w_out[l])
    return x
```

```python
import functools
import math

import jax
import jax.numpy as jnp
from jax import lax
from jax.experimental import pallas as pl
from jax.experimental.pallas import tpu as pltpu

F32 = jnp.float32
BF16 = jnp.bfloat16

HEAD_DIM = 64
CONV_KERNEL = 31
EPS = 1e-6
NEG_INF = -1e30
LOG2E = math.log2(math.e)

LANES = 128
HEADS_PER_STEP = LANES // HEAD_DIM
VMEM_LIMIT = 56 * 1024 * 1024

ROW_TILE = 512
ATT_TILE = 256
CONV_TILE = 256
CONV_HALO = 32


def _split3(x):
    hi = x.astype(BF16)
    r1 = x - hi.astype(F32)
    mid = r1.astype(BF16)
    lo = (r1 - mid.astype(F32)).astype(BF16)
    return hi, mid, lo


def _sigmoid(x):
    return 1.0 / (1.0 + jnp.exp(-x))


def _inproj_kernel(x_ref, g_ref, w_ref, wf_ref, bf_ref, tri_ref,
                   q_ref, k_ref, v_ref, sgf_ref, u_ref, sgc_ref, c_ref,
                   carry_ref, *, tiles_per_seq, width):
    i = pl.program_id(0)
    x = x_ref[...]
    ms = jnp.mean(x * x, axis=-1, keepdims=True)
    h = (x * lax.rsqrt(ms + EPS) * g_ref[...]).astype(BF16)

    def mm(sec):
        return jnp.dot(h, w_ref[:, sec * width:(sec + 1) * width],
                       preferred_element_type=F32)

    q_ref[...] = mm(0).astype(BF16)
    k_ref[...] = mm(1).astype(BF16)
    v_ref[...] = mm(2).astype(BF16)
    gf = mm(3)
    sgf_ref[...] = (gf * _sigmoid(gf)).astype(BF16)
    glu_a = mm(4)
    glu_b = mm(5)
    u_ref[...] = (glu_a * _sigmoid(glu_b)).astype(BF16)
    gc = mm(6)
    sgc_ref[...] = (gc * _sigmoid(gc)).astype(BF16)

    zf = jnp.dot(h, wf_ref[...], preferred_element_type=F32) + bf_ref[...]
    log_f = jnp.minimum(zf, 0.0) - jnp.log1p(jnp.exp(-jnp.abs(zf)))
    pieces = jnp.concatenate(_split3(log_f), axis=1)
    cs = jnp.dot(tri_ref[...], pieces, preferred_element_type=F32)
    c_tile = cs[:, :LANES] + cs[:, LANES:2 * LANES] + cs[:, 2 * LANES:]

    @pl.when(i % tiles_per_seq == 0)
    def _():
        carry_ref[...] = jnp.zeros_like(carry_ref)

    c_full = c_tile + carry_ref[...]
    c_ref[...] = c_full
    carry_ref[...] = c_full[c_full.shape[0] - 1:, :]


def _inproj(x2d, norm_g, w_main, w_f, b_f, seq_len):
    m, d = x2d.shape
    tm = ROW_TILE
    width = d
    tri = (lax.broadcasted_iota(jnp.int32, (tm, tm), 0)
           >= lax.broadcasted_iota(jnp.int32, (tm, tm), 1)).astype(BF16)
    row = lambda i: (i, 0)
    const = lambda i: (0, 0)
    out_bf = jax.ShapeDtypeStruct((m, width), BF16)
    return pl.pallas_call(
        functools.partial(_inproj_kernel, tiles_per_seq=seq_len // tm, width=width),
        grid=(m // tm,),
        in_specs=[
            pl.BlockSpec((tm, d), row),
            pl.BlockSpec((1, d), const),
            pl.BlockSpec(w_main.shape, const, pipeline_mode=pl.Buffered(1)),
            pl.BlockSpec(w_f.shape, const),
            pl.BlockSpec((1, LANES), const),
            pl.BlockSpec((tm, tm), const),
        ],
        out_specs=[pl.BlockSpec((tm, width), row)] * 6 + [pl.BlockSpec((tm, LANES), row)],
        out_shape=[out_bf] * 6 + [jax.ShapeDtypeStruct((m, LANES), F32)],
        scratch_shapes=[pltpu.VMEM((1, LANES), F32)],
        compiler_params=pltpu.CompilerParams(
            dimension_semantics=("arbitrary",), vmem_limit_bytes=VMEM_LIMIT),
        name="inproj",
    )(x2d, norm_g, w_main, w_f, b_f, tri)


def _attn_kernel(q_ref, k_ref, v_ref, c_ref, gate_ref, gq_ref, gk_ref, o_ref,
                 kaug_ref, qaugt_ref, vt_ref):
    t = ATT_TILE
    seq = q_ref.shape[0]
    n_tiles = seq // t
    pair = pl.program_id(1)

    lane = lax.broadcasted_iota(jnp.int32, (1, LANES), 1)
    lo_half = lane < HEAD_DIM
    sub = lax.broadcasted_iota(jnp.int32, (LANES, 1), 0)

    def head_norm(xf, gain):
        sq = xf * xf
        s_lo = jnp.sum(jnp.where(lo_half, sq, 0.0), axis=-1, keepdims=True)
        s_hi = jnp.sum(jnp.where(lo_half, 0.0, sq), axis=-1, keepdims=True)
        mean_sq = jnp.where(lo_half, s_lo, s_hi) * (1.0 / HEAD_DIM)
        return xf * lax.rsqrt(mean_sq + EPS) * gain

    pr = lax.broadcasted_iota(jnp.int32, (3 * LANES, LANES), 0)
    pc = lax.broadcasted_iota(jnp.int32, (3 * LANES, LANES), 1)
    qr = lax.broadcasted_iota(jnp.int32, (LANES, 3 * LANES), 0)
    qc = lax.broadcasted_iota(jnp.int32, (LANES, 3 * LANES), 1)
    place_k, place_qt = [], []
    for h in range(HEADS_PER_STEP):
        head = pair * HEADS_PER_STEP + h
        place_k.append(jnp.where((pc >= 3) & (pc < 6) & (pr == head + LANES * (pc - 3)),
                                 -1.0, 0.0).astype(BF16))
        place_qt.append(jnp.where((qr < 3) & (qc == head + LANES * qr), 1.0, 0.0).astype(BF16))
    ones_k = jnp.where(lane < 3, 1.0, 0.0)
    ones_qt = jnp.where((sub >= 3) & (sub < 6), 1.0, 0.0)

    def prologue(r, _):
        rows = pl.ds(pl.multiple_of(r * t, t), t)
        kn = head_norm(k_ref[rows, :].astype(F32), gk_ref[...]).astype(BF16)
        qn = head_norm(q_ref[rows, :].astype(F32), gq_ref[...]) * (LOG2E * HEAD_DIM ** -0.5)
        qnt = qn.T
        pieces = jnp.concatenate(_split3(c_ref[rows, :] * LOG2E), axis=1)
        for h in range(HEADS_PER_STEP):
            kaug_ref[h, rows, 0:LANES] = kn
            kdec = jnp.dot(pieces, place_k[h], preferred_element_type=F32) + ones_k
            kaug_ref[h, rows, LANES:2 * LANES] = kdec.astype(BF16)
            in_head = (sub >= h * HEAD_DIM) & (sub < (h + 1) * HEAD_DIM)
            qaugt_ref[h, r, 0:LANES, :] = jnp.where(in_head, qnt, 0.0).astype(BF16)
            qdec_t = lax.dot_general(place_qt[h], pieces, (((1,), (1,)), ((), ())),
                                     preferred_element_type=F32) + ones_qt
            qaugt_ref[h, r, LANES:2 * LANES, :] = qdec_t.astype(BF16)
        vt_ref[r] = v_ref[rows, :].astype(F32).T.astype(BF16)
        return 0

    lax.fori_loop(0, n_tiles, prologue, 0)

    k_idx = lax.broadcasted_iota(jnp.int32, (t, t), 0)
    q_idx = lax.broadcasted_iota(jnp.int32, (t, t), 1)
    causal = k_idx <= q_idx

    def q_tile(i, _):
        def kv_step(j, carry, masked):
            ms, ls, acc = carry
            new_m, new_l, new_acc = [], [], []
            krows = pl.ds(pl.multiple_of(j * t, t), t)
            for h in range(HEADS_PER_STEP):
                s_t = jnp.dot(kaug_ref[h, krows, :], qaugt_ref[h, i],
                              preferred_element_type=F32)
                if masked:
                    s_t = jnp.where(causal, s_t, NEG_INF)
                m_new = jnp.maximum(ms[h], jnp.max(s_t, axis=0, keepdims=True))
                p = jnp.exp2(s_t - m_new)
                alpha = jnp.exp2(ms[h] - m_new)
                new_m.append(m_new)
                new_l.append(alpha * ls[h] + jnp.sum(p, axis=0, keepdims=True))
                v_h = vt_ref[j][h * HEAD_DIM:(h + 1) * HEAD_DIM, :]
                pv = jnp.dot(v_h, p.astype(BF16), preferred_element_type=F32)
                new_acc.append(alpha * acc[h] + pv)
            return tuple(new_m), tuple(new_l), tuple(new_acc)

        init = (
            tuple(jnp.full((1, t), NEG_INF, F32) for _ in range(HEADS_PER_STEP)),
            tuple(jnp.zeros((1, t), F32) for _ in range(HEADS_PER_STEP)),
            tuple(jnp.zeros((HEAD_DIM, t), F32) for _ in range(HEADS_PER_STEP)),
        )
        carry = lax.fori_loop(0, i, functools.partial(kv_step, masked=False), init)
        _, ls, acc = kv_step(i, carry, masked=True)
        out_t = jnp.concatenate([acc[h] / ls[h] for h in range(HEADS_PER_STEP)], axis=0)
        rows = pl.ds(pl.multiple_of(i * t, t), t)
        o_ref[rows, :] = (out_t.T * gate_ref[rows, :].astype(F32)).astype(BF16)
        return 0

    lax.fori_loop(0, n_tiles, q_tile, 0)


def _attention(q, k, v, c, gate, gq, gk, batch, seq_len):
    m, width = q.shape
    n_pairs = width // LANES
    t = ATT_TILE
    blocks_per_seq = 1
    slab = lambda b, p: (b, p)
    spec = pl.BlockSpec((seq_len, LANES), slab)
    return pl.pallas_call(
        _attn_kernel,
        grid=(batch, n_pairs),
        in_specs=[spec, spec, spec,
                  pl.BlockSpec((seq_len, LANES), lambda b, p: (b, 0)),
                  spec,
                  pl.BlockSpec((1, LANES), lambda b, p: (0, p)),
                  pl.BlockSpec((1, LANES), lambda b, p: (0, p))],
        out_specs=spec,
        out_shape=jax.ShapeDtypeStruct((m, width), BF16),
        scratch_shapes=[
            pltpu.VMEM((HEADS_PER_STEP, seq_len, 2 * LANES), BF16),
            pltpu.VMEM((HEADS_PER_STEP, seq_len // t, 2 * LANES, t), BF16),
            pltpu.VMEM((seq_len // t, LANES, t), BF16),
        ],
        compiler_params=pltpu.CompilerParams(
            dimension_semantics=("arbitrary", "arbitrary"), vmem_limit_bytes=VMEM_LIMIT),
        name="fox_attention",
    )(q, k, v, c, gate, gq, gk)


def _conv_kernel(u_ref, gate_ref, w_ref, b_ref, lng_ref, lnb_ref, o_ref, ext_ref, y_ref):
    ts = u_ref.shape[0]
    s_idx = pl.program_id(1)

    @pl.when(s_idx == 0)
    def _():
        ext_ref[0:CONV_HALO, :] = jnp.zeros((CONV_HALO, ext_ref.shape[1]), F32)

    @pl.when(s_idx != 0)
    def _():
        ext_ref[0:CONV_HALO, :] = ext_ref[ts:ts + CONV_HALO, :]

    ext_ref[CONV_HALO:CONV_HALO + ts, :] = u_ref[...].astype(F32)

    rc = 64
    first = CONV_HALO - (CONV_KERNEL - 1)
    for lt in range(ext_ref.shape[1] // LANES):
        cols = slice(lt * LANES, (lt + 1) * LANES)
        for r0 in range(0, ts, rc):
            acc = jnp.zeros((rc, LANES), F32) + b_ref[:, cols]
            for j in range(CONV_KERNEL):
                acc = acc + ext_ref[first + j + r0:first + j + r0 + rc, cols] * w_ref[j:j + 1, cols]
            y_ref[r0:r0 + rc, cols] = acc

    y = y_ref[...]
    mu = jnp.mean(y, axis=-1, keepdims=True)
    yc = y - mu
    var = jnp.mean(yc * yc, axis=-1, keepdims=True)
    z = yc * lax.rsqrt(var + EPS) * lng_ref[...] + lnb_ref[...]
    o_ref[...] = (z * _sigmoid(z) * gate_ref[...].astype(F32)).astype(BF16)


def _conv_branch(u, gate, conv_w, conv_b, ln_g, ln_b, batch, seq_len):
    m, width = u.shape
    ts = CONV_TILE
    n_s = seq_len // ts
    row = lambda b, s: (b * n_s + s, 0)
    const = lambda b, s: (0, 0)
    return pl.pallas_call(
        _conv_kernel,
        grid=(batch, n_s),
        in_specs=[pl.BlockSpec((ts, width), row), pl.BlockSpec((ts, width), row),
                  pl.BlockSpec(conv_w.shape, const), pl.BlockSpec((1, width), const),
                  pl.BlockSpec((1, width), const), pl.BlockSpec((1, width), const)],
        out_specs=pl.BlockSpec((ts, width), row),
        out_shape=jax.ShapeDtypeStruct((m, width), BF16),
        scratch_shapes=[pltpu.VMEM((CONV_HALO + ts, width), F32),
                        pltpu.VMEM((ts, width), F32)],
        compiler_params=pltpu.CompilerParams(
            dimension_semantics=("arbitrary", "arbitrary"), vmem_limit_bytes=VMEM_LIMIT),
        name="conv_branch",
    )(u, gate, conv_w, conv_b, ln_g, ln_b)


def _outproj_kernel(x_ref, a_ref, u_ref, w_ref, o_ref):
    half = a_ref.shape[1]
    o_ref[...] = (x_ref[...]
                  + jnp.dot(a_ref[...], w_ref[0:half, :], preferred_element_type=F32)
                  + jnp.dot(u_ref[...], w_ref[half:, :], preferred_element_type=F32))


def _outproj(x2d, a, u, w_out):
    m, d = x2d.shape
    tm = ROW_TILE
    row = lambda i: (i, 0)
    return pl.pallas_call(
        _outproj_kernel,
        grid=(m // tm,),
        in_specs=[pl.BlockSpec((tm, d), row), pl.BlockSpec((tm, a.shape[1]), row),
                  pl.BlockSpec((tm, u.shape[1]), row),
                  pl.BlockSpec(w_out.shape, lambda i: (0, 0))],
        out_specs=pl.BlockSpec((tm, d), row),
        out_shape=jax.ShapeDtypeStruct((m, d), F32),
        compiler_params=pltpu.CompilerParams(
            dimension_semantics=("arbitrary",), vmem_limit_bytes=VMEM_LIMIT),
        name="outproj",
    )(x2d, a, u, w_out)


def kernel(x, norm_g, w_in, b_forget, q_norm_g, k_norm_g, conv_w, conv_b, conv_ln_g, conv_ln_b, w_out):
    batch, seq_len, d_model = x.shape
    depth = w_in.shape[0]
    n_heads = b_forget.shape[1]
    fox_width = n_heads * HEAD_DIM
    conv_width = conv_w.shape[2]
    assert fox_width == d_model and conv_width == d_model
    assert seq_len % ROW_TILE == 0 and seq_len % ATT_TILE == 0 and seq_len % CONV_TILE == 0
    assert n_heads <= LANES and CONV_HALO >= CONV_KERNEL - 1

    o_f = 3 * fox_width
    o_gf = o_f + n_heads
    x2d = x.reshape(batch * seq_len, d_model)
    for l in range(depth):
        w = w_in[l]
        w_main = jnp.concatenate([w[:, :o_f], w[:, o_gf:]], axis=1).astype(BF16)
        w_f = jnp.pad(w[:, o_f:o_gf], ((0, 0), (0, LANES - n_heads))).astype(BF16)
        b_f = jnp.pad(b_forget[l], (0, LANES - n_heads)).reshape(1, LANES)
        q, k, v, sgf, u, sgc, c = _inproj(x2d, norm_g[l].reshape(1, d_model), w_main, w_f, b_f, seq_len)
        a = _attention(q, k, v, c, sgf, q_norm_g[l].reshape(1, fox_width),
                       k_norm_g[l].reshape(1, fox_width), batch, seq_len)
        yu = _conv_branch(u, sgc, conv_w[l], conv_b[l].reshape(1, conv_width),
                          conv_ln_g[l].reshape(1, conv_width), conv_ln_b[l].reshape(1, conv_width),
                          batch, seq_len)
        x2d = _outproj(x2d, a, yu, w_out[l].astype(BF16))
    return x2d.reshape(batch, seq_len, d_model)
```

```python
import functools
import math

import jax
import jax.numpy as jnp
from jax import lax
from jax.experimental import pallas as pl
from jax.experimental.pallas import tpu as pltpu

F32 = jnp.float32
BF16 = jnp.bfloat16

HEAD_DIM = 64
CONV_KERNEL = 31
EPS = 1e-6
NEG_INF = -1e30
LOG2E = math.log2(math.e)

LANES = 128
SUBLANES = 8
BF16_ROWS = 16
HEADS_PER_STEP = LANES // HEAD_DIM
VMEM_LIMIT = 56 * 1024 * 1024

ROW_TILE = 512
ATT_TILE = 256
ATT_PIPE = 4
V_ROWS = HEAD_DIM + BF16_ROWS
CONV_TILE = 256
CONV_HALO = 32
CONV_CHUNK = 64


def _split3(x):
    hi = x.astype(BF16)
    r1 = x - hi.astype(F32)
    mid = r1.astype(BF16)
    lo = (r1 - mid.astype(F32)).astype(BF16)
    return hi, mid, lo


def _sigmoid(x):
    return 1.0 / (1.0 + jnp.exp(-x))


def _inproj_kernel(x_ref, g_ref, w_ref, wf_ref, bf_ref, tri_ref,
                   q_ref, k_ref, v_ref, sgf_ref, u_ref, sgc_ref, c_ref,
                   carry_ref, *, tiles_per_seq, width):
    i = pl.program_id(0)
    x = x_ref[...]
    ms = jnp.mean(x * x, axis=-1, keepdims=True)
    h = (x * lax.rsqrt(ms + EPS) * g_ref[...]).astype(BF16)

    def mm(sec):
        return jnp.dot(h, w_ref[:, sec * width:(sec + 1) * width],
                       preferred_element_type=F32)

    q_ref[...] = mm(0).astype(BF16)
    k_ref[...] = mm(1).astype(BF16)
    v_ref[...] = mm(2).astype(BF16)
    gf = mm(3)
    sgf_ref[...] = (gf * _sigmoid(gf)).astype(BF16)
    glu_a = mm(4)
    glu_b = mm(5)
    u_ref[...] = (glu_a * _sigmoid(glu_b)).astype(BF16)
    gc = mm(6)
    sgc_ref[...] = (gc * _sigmoid(gc)).astype(BF16)

    zf = jnp.dot(h, wf_ref[...], preferred_element_type=F32) + bf_ref[...]
    log_f = jnp.minimum(zf, 0.0) - jnp.log1p(jnp.exp(-jnp.abs(zf)))
    pieces = jnp.concatenate(_split3(log_f), axis=1)
    cs = jnp.dot(tri_ref[...], pieces, preferred_element_type=F32)
    c_tile = cs[:, :LANES] + cs[:, LANES:2 * LANES] + cs[:, 2 * LANES:]

    @pl.when(i % tiles_per_seq == 0)
    def _():
        carry_ref[...] = jnp.zeros_like(carry_ref)

    c_full = c_tile + carry_ref[...]
    c_ref[...] = c_full
    carry_ref[...] = c_full[c_full.shape[0] - 1:, :]


def _inproj(x2d, norm_g, w_main, w_f, b_f, seq_len):
    m, d = x2d.shape
    tm = ROW_TILE
    width = d
    tri = (lax.broadcasted_iota(jnp.int32, (tm, tm), 0)
           >= lax.broadcasted_iota(jnp.int32, (tm, tm), 1)).astype(BF16)
    row = lambda i: (i, 0)
    const = lambda i: (0, 0)
    out_bf = jax.ShapeDtypeStruct((m, width), BF16)
    return pl.pallas_call(
        functools.partial(_inproj_kernel, tiles_per_seq=seq_len // tm, width=width),
        grid=(m // tm,),
        in_specs=[
            pl.BlockSpec((tm, d), row),
            pl.BlockSpec((1, d), const),
            pl.BlockSpec(w_main.shape, const, pipeline_mode=pl.Buffered(1)),
            pl.BlockSpec(w_f.shape, const),
            pl.BlockSpec((1, LANES), const),
            pl.BlockSpec((tm, tm), const),
        ],
        out_specs=[pl.BlockSpec((tm, width), row)] * 6 + [pl.BlockSpec((tm, LANES), row)],
        out_shape=[out_bf] * 6 + [jax.ShapeDtypeStruct((m, LANES), F32)],
        scratch_shapes=[pltpu.VMEM((1, LANES), F32)],
        compiler_params=pltpu.CompilerParams(
            dimension_semantics=("arbitrary",), vmem_limit_bytes=VMEM_LIMIT),
        name="inproj",
    )(x2d, norm_g, w_main, w_f, b_f, tri)


def _attn_kernel(q_ref, k_ref, v_ref, c_ref, gate_ref, gq_ref, gk_ref, o_ref,
                 kaug_ref, qaugt_ref, vt_ref, s_ref, bias_ref, acc_ref, m_ref):
    t = ATT_TILE
    seq = q_ref.shape[0]
    n_tiles = seq // t
    pair = pl.program_id(1)

    lane = lax.broadcasted_iota(jnp.int32, (1, LANES), 1)
    lo_half = lane < HEAD_DIM
    sub = lax.broadcasted_iota(jnp.int32, (LANES, 1), 0)

    def head_norm(xf, gain):
        sq = xf * xf
        s_lo = jnp.sum(jnp.where(lo_half, sq, 0.0), axis=-1, keepdims=True)
        s_hi = jnp.sum(jnp.where(lo_half, 0.0, sq), axis=-1, keepdims=True)
        mean_sq = jnp.where(lo_half, s_lo, s_hi) * (1.0 / HEAD_DIM)
        return xf * lax.rsqrt(mean_sq + EPS) * gain

    pr = lax.broadcasted_iota(jnp.int32, (3 * LANES, LANES), 0)
    pc = lax.broadcasted_iota(jnp.int32, (3 * LANES, LANES), 1)
    qr = lax.broadcasted_iota(jnp.int32, (LANES, 3 * LANES), 0)
    qc = lax.broadcasted_iota(jnp.int32, (LANES, 3 * LANES), 1)
    place_k, place_qt = [], []
    for h in range(HEADS_PER_STEP):
        head = pair * HEADS_PER_STEP + h
        place_k.append(jnp.where((pc >= 3) & (pc < 6) & (pr == head + LANES * (pc - 3)),
                                 -1.0, 0.0).astype(BF16))
        place_qt.append(jnp.where((qr < 3) & (qc == head + LANES * qr), 1.0, 0.0).astype(BF16))
    ones_k = jnp.where(lane < 3, 1.0, 0.0)
    ones_qt = jnp.where((sub >= 3) & (sub < 6), 1.0, 0.0)

    def prologue(r, _):
        rows = pl.ds(pl.multiple_of(r * t, t), t)
        kn = head_norm(k_ref[rows, :].astype(F32), gk_ref[...]).astype(BF16)
        qn = head_norm(q_ref[rows, :].astype(F32), gq_ref[...]) * (LOG2E * HEAD_DIM ** -0.5)
        qnt = qn.T
        pieces = jnp.concatenate(_split3(c_ref[rows, :] * LOG2E), axis=1)
        for h in range(HEADS_PER_STEP):
            kaug_ref[h, rows, 0:LANES] = kn
            kdec = jnp.dot(pieces, place_k[h], preferred_element_type=F32) + ones_k
            kaug_ref[h, rows, LANES:2 * LANES] = kdec.astype(BF16)
            in_head = (sub >= h * HEAD_DIM) & (sub < (h + 1) * HEAD_DIM)
            qaugt_ref[h, r, 0:LANES, :] = jnp.where(in_head, qnt, 0.0).astype(BF16)
            qdec_t = lax.dot_general(place_qt[h], pieces, (((1,), (1,)), ((), ())),
                                     preferred_element_type=F32) + ones_qt
            qaugt_ref[h, r, LANES:2 * LANES, :] = qdec_t.astype(BF16)
        v_t = v_ref[rows, :].astype(F32).T
        for h in range(HEADS_PER_STEP):
            vt_ref[r, h * V_ROWS:h * V_ROWS + HEAD_DIM, :] = v_t[h * HEAD_DIM:(h + 1) * HEAD_DIM, :].astype(BF16)
            vt_ref[r, h * V_ROWS + HEAD_DIM:(h + 1) * V_ROWS, :] = jnp.ones((V_ROWS - HEAD_DIM, t), BF16)
        return 0

    lax.fori_loop(0, n_tiles, prologue, 0)

    k_idx = lax.broadcasted_iota(jnp.int32, (t, t), 0)
    q_idx = lax.broadcasted_iota(jnp.int32, (t, t), 1)
    bias_ref[0] = jnp.full((t, t), jnp.inf, F32)
    bias_ref[1] = jnp.where(k_idx <= q_idx, jnp.inf, NEG_INF)
    acc_ref[...] = jnp.zeros_like(acc_ref)
    m_ref[...] = jnp.full(m_ref.shape, NEG_INF, F32)

    def qk(i, j):
        krows = pl.ds(pl.multiple_of(j * t, t), t)
        return [jnp.dot(kaug_ref[h, krows, :], qaugt_ref[h, i], preferred_element_type=F32)
                for h in range(HEADS_PER_STEP)]

    def store_scores(slot, i, j, s):
        bias = bias_ref[(i == j).astype(jnp.int32)]
        blk_max = []
        for h in range(HEADS_PER_STEP):
            s_h = jnp.minimum(s[h], bias)
            blk_max.append(jnp.max(s_h, axis=0, keepdims=True))
            s_ref[slot, h] = s_h
        return blk_max

    def softmax_pv(slot, i, j, blk_max):
        for h in range(HEADS_PER_STEP):
            m_old = m_ref[i, h:h + 1, :]
            m_new = jnp.maximum(m_old, blk_max[h])
            p = jnp.exp2(s_ref[slot, h] - m_new)
            alpha = jnp.exp2(m_old - m_new)
            hrows = slice(h * V_ROWS, (h + 1) * V_ROWS)
            pv = jnp.dot(vt_ref[j, hrows, :], p.astype(BF16), preferred_element_type=F32)
            acc_ref[i, hrows, :] = alpha * acc_ref[i, hrows, :] + pv
            m_ref[i, h:h + 1, :] = m_new

    n_blocks = n_tiles * (n_tiles + 1) // 2
    assert n_blocks % ATT_PIPE == 0

    def advance(i, j):
        wrap = j >= i
        return jnp.where(wrap, i + 1, i), jnp.where(wrap, 0, j + 1)

    ij, maxes = [], []
    cur = (jnp.int32(0), jnp.int32(0))
    for slot in range(ATT_PIPE):
        ij.append(cur)
        maxes.append(store_scores(slot, *cur, qk(*cur)))
        cur = advance(*cur)

    def trip(_, carry):
        ij, maxes = carry
        nxt = ij[-1]
        new_ij, new_maxes = [], []
        for slot in range(ATT_PIPE):
            nxt = advance(*nxt)
            s_next = qk(jnp.minimum(nxt[0], n_tiles - 1), nxt[1])
            softmax_pv(slot, *ij[slot], maxes[slot])
            new_maxes.append(store_scores(slot, *nxt, s_next))
            new_ij.append(nxt)
        return new_ij, new_maxes

    lax.fori_loop(0, n_blocks // ATT_PIPE, trip, (ij, maxes))

    def finalize(i, _):
        parts = []
        for h in range(HEADS_PER_STEP):
            row_sum = acc_ref[i, h * V_ROWS + HEAD_DIM:h * V_ROWS + HEAD_DIM + 1, :]
            parts.append(acc_ref[i, h * V_ROWS:h * V_ROWS + HEAD_DIM, :] * (1.0 / row_sum))
        rows = pl.ds(pl.multiple_of(i * t, t), t)
        out_t = jnp.concatenate(parts, axis=0)
        o_ref[rows, :] = (out_t.T * gate_ref[rows, :].astype(F32)).astype(BF16)
        return 0

    lax.fori_loop(0, n_tiles, finalize, 0)


def _attention(q, k, v, c, gate, gq, gk, batch, seq_len):
    m, width = q.shape
    n_pairs = width // LANES
    t = ATT_TILE
    slab = lambda b, p: (b, p)
    spec = pl.BlockSpec((seq_len, LANES), slab)
    return pl.pallas_call(
        _attn_kernel,
        grid=(batch, n_pairs),
        in_specs=[spec, spec, spec,
                  pl.BlockSpec((seq_len, LANES), lambda b, p: (b, 0)),
                  spec,
                  pl.BlockSpec((1, LANES), lambda b, p: (0, p)),
                  pl.BlockSpec((1, LANES), lambda b, p: (0, p))],
        out_specs=spec,
        out_shape=jax.ShapeDtypeStruct((m, width), BF16),
        scratch_shapes=[
            pltpu.VMEM((HEADS_PER_STEP, seq_len, 2 * LANES), BF16),
            pltpu.VMEM((HEADS_PER_STEP, seq_len // t, 2 * LANES, t), BF16),
            pltpu.VMEM((seq_len // t, HEADS_PER_STEP * V_ROWS, t), BF16),
            pltpu.VMEM((ATT_PIPE, HEADS_PER_STEP, t, t), F32),
            pltpu.VMEM((2, t, t), F32),
            pltpu.VMEM((seq_len // t, HEADS_PER_STEP * V_ROWS, t), F32),
            pltpu.VMEM((seq_len // t, HEADS_PER_STEP, t), F32),
        ],
        compiler_params=pltpu.CompilerParams(
            dimension_semantics=("arbitrary", "arbitrary"), vmem_limit_bytes=VMEM_LIMIT),
        name="fox_attention",
    )(q, k, v, c, gate, gq, gk)


def _conv_kernel(u_ref, gate_ref, w_ref, b_ref, lng_ref, lnb_ref, o_ref, ext_ref, sh_ref, y_ref):
    ts = u_ref.shape[0]
    n_lt = ext_ref.shape[0]
    s_idx = pl.program_id(1)

    @pl.when(s_idx == 0)
    def _():
        ext_ref[:, 0:CONV_HALO, :] = jnp.zeros((n_lt, CONV_HALO, LANES), F32)

    @pl.when(s_idx != 0)
    def _():
        ext_ref[:, 0:CONV_HALO, :] = ext_ref[:, ts:ts + CONV_HALO, :]

    for lt in range(n_lt):
        ext_ref[lt, CONV_HALO:CONV_HALO + ts, :] = u_ref[:, lt * LANES:(lt + 1) * LANES].astype(F32)

    rc = CONV_CHUNK
    first = CONV_HALO - (CONV_KERNEL - 1)
    sh_rows = sh_ref.shape[1]

    def lane_tile(lt, _):
        for b in range(1, SUBLANES):
            sh_ref[b - 1] = ext_ref[lt, b:b + sh_rows, :]

        def chunk(c, _):
            r0 = pl.multiple_of(c * rc, rc)
            acc = jnp.zeros((rc, LANES), F32) + b_ref[lt]
            for j in range(CONV_KERNEL):
                a, b = divmod(first + j, SUBLANES)
                rows = pl.ds(r0 + a * SUBLANES, rc)
                tap = ext_ref[lt, rows, :] if b == 0 else sh_ref[b - 1, rows, :]
                acc = acc + tap * w_ref[lt, j:j + 1, :]
            y_ref[lt, pl.ds(r0, rc), :] = acc
            return 0

        lax.fori_loop(0, ts // rc, chunk, 0)
        return 0

    lax.fori_loop(0, n_lt, lane_tile, 0)

    y = jnp.concatenate([y_ref[lt] for lt in range(n_lt)], axis=1)
    mu = jnp.mean(y, axis=-1, keepdims=True)
    yc = y - mu
    var = jnp.mean(yc * yc, axis=-1, keepdims=True)
    z = yc * lax.rsqrt(var + EPS) * lng_ref[...] + lnb_ref[...]
    o_ref[...] = (z * _sigmoid(z) * gate_ref[...].astype(F32)).astype(BF16)


def _conv_branch(u, gate, conv_w, conv_b, ln_g, ln_b, batch, seq_len):
    m, width = u.shape
    ts = CONV_TILE
    n_s = seq_len // ts
    n_lt = width // LANES
    w_tiles = conv_w.reshape(CONV_KERNEL, n_lt, LANES).transpose(1, 0, 2)
    b_tiles = conv_b.reshape(n_lt, 1, LANES)
    row = lambda b, s: (b * n_s + s, 0)
    const2 = lambda b, s: (0, 0)
    const3 = lambda b, s: (0, 0, 0)
    return pl.pallas_call(
        _conv_kernel,
        grid=(batch, n_s),
        in_specs=[pl.BlockSpec((ts, width), row), pl.BlockSpec((ts, width), row),
                  pl.BlockSpec(w_tiles.shape, const3), pl.BlockSpec(b_tiles.shape, const3),
                  pl.BlockSpec((1, width), const2), pl.BlockSpec((1, width), const2)],
        out_specs=pl.BlockSpec((ts, width), row),
        out_shape=jax.ShapeDtypeStruct((m, width), BF16),
        scratch_shapes=[pltpu.VMEM((n_lt, CONV_HALO + ts, LANES), F32),
                        pltpu.VMEM((SUBLANES - 1, CONV_HALO + ts - SUBLANES, LANES), F32),
                        pltpu.VMEM((n_lt, ts, LANES), F32)],
        compiler_params=pltpu.CompilerParams(
            dimension_semantics=("arbitrary", "arbitrary"), vmem_limit_bytes=VMEM_LIMIT),
        name="conv_branch",
    )(u, gate, w_tiles, b_tiles, ln_g, ln_b)


def _outproj_kernel(x_ref, a_ref, u_ref, w_ref, o_ref):
    half = a_ref.shape[1]
    o_ref[...] = (x_ref[...]
                  + jnp.dot(a_ref[...], w_ref[0:half, :], preferred_element_type=F32)
                  + jnp.dot(u_ref[...], w_ref[half:, :], preferred_element_type=F32))


def _outproj(x2d, a, u, w_out):
    m, d = x2d.shape
    tm = ROW_TILE
    row = lambda i: (i, 0)
    return pl.pallas_call(
        _outproj_kernel,
        grid=(m // tm,),
        in_specs=[pl.BlockSpec((tm, d), row), pl.BlockSpec((tm, a.shape[1]), row),
                  pl.BlockSpec((tm, u.shape[1]), row),
                  pl.BlockSpec(w_out.shape, lambda i: (0, 0))],
        out_specs=pl.BlockSpec((tm, d), row),
        out_shape=jax.ShapeDtypeStruct((m, d), F32),
        compiler_params=pltpu.CompilerParams(
            dimension_semantics=("arbitrary",), vmem_limit_bytes=VMEM_LIMIT),
        name="outproj",
    )(x2d, a, u, w_out)


def kernel(x, norm_g, w_in, b_forget, q_norm_g, k_norm_g, conv_w, conv_b, conv_ln_g, conv_ln_b, w_out):
    batch, seq_len, d_model = x.shape
    depth = w_in.shape[0]
    n_heads = b_forget.shape[1]
    fox_width = n_heads * HEAD_DIM
    conv_width = conv_w.shape[2]
    assert fox_width == d_model and conv_width == d_model
    assert seq_len % ROW_TILE == 0 and seq_len % ATT_TILE == 0 and seq_len % CONV_TILE == 0
    assert n_heads <= LANES and CONV_HALO >= CONV_KERNEL - 1

    o_f = 3 * fox_width
    o_gf = o_f + n_heads
    x2d = x.reshape(batch * seq_len, d_model)
    for l in range(depth):
        w = w_in[l]
        w_main = jnp.concatenate([w[:, :o_f], w[:, o_gf:]], axis=1).astype(BF16)
        w_f = jnp.pad(w[:, o_f:o_gf], ((0, 0), (0, LANES - n_heads))).astype(BF16)
        b_f = jnp.pad(b_forget[l], (0, LANES - n_heads)).reshape(1, LANES)
        q, k, v, sgf, u, sgc, c = _inproj(x2d, norm_g[l].reshape(1, d_model), w_main, w_f, b_f, seq_len)
        a = _attention(q, k, v, c, sgf, q_norm_g[l].reshape(1, fox_width),
                       k_norm_g[l].reshape(1, fox_width), batch, seq_len)
        yu = _conv_branch(u, sgc, conv_w[l], conv_b[l],
                          conv_ln_g[l].reshape(1, conv_width), conv_ln_b[l].reshape(1, conv_width),
                          batch, seq_len)
        x2d = _outproj(x2d, a, yu, w_out[l].astype(BF16))
    return x2d.reshape(batch, seq_len, d_model)
```

```python
import functools
import math

import jax
import jax.numpy as jnp
import numpy as np
from jax import lax
from jax.experimental import pallas as pl
from jax.experimental.pallas import tpu as pltpu

F32 = jnp.float32
BF16 = jnp.bfloat16

HEAD_DIM = 64
CONV_KERNEL = 31
EPS = 1e-6
NEG_INF = -1e30
LOG2E = math.log2(math.e)

LANES = 128
SUBLANES = 8
BF16_ROWS = 16
HEADS_PER_STEP = LANES // HEAD_DIM
DECAY_LANES = 8
VMEM_LIMIT = 56 * 1024 * 1024

ROW_TILE = 512
ATT_TILE = 256
ATT_PIPE = 4
V_ROWS = HEAD_DIM + BF16_ROWS
CONV_TILE = 256
CONV_HALO = 32
CONV_CHUNK = 128


def _split3(x):
    hi = x.astype(BF16)
    r1 = x - hi.astype(F32)
    mid = r1.astype(BF16)
    lo = (r1 - mid.astype(F32)).astype(BF16)
    return hi, mid, lo


def _sigmoid(x):
    return 0.5 * jnp.tanh(0.5 * x) + 0.5


def _silu(x):
    h = 0.5 * x
    return h + h * jnp.tanh(h)


def _head_norm(xf, gain):
    lo_half = lax.broadcasted_iota(jnp.int32, (1, LANES), 1) < HEAD_DIM
    sq = xf * xf
    s_lo = jnp.sum(jnp.where(lo_half, sq, 0.0), axis=-1, keepdims=True)
    s_hi = jnp.sum(jnp.where(lo_half, 0.0, sq), axis=-1, keepdims=True)
    mean_sq = jnp.where(lo_half, s_lo, s_hi) * (1.0 / HEAD_DIM)
    return xf * lax.rsqrt(mean_sq + EPS) * gain


def _inproj_kernel(x_ref, g_ref, w_ref, wf_ref, bf_ref, tri_ref, gq_ref, gk_ref, pk_ref, pq_ref,
                   q_ref, k_ref, v_ref, sgf_ref, u_ref, sgc_ref, kdec_ref, qdec_ref,
                   carry_ref, *, tiles_per_seq, width):
    i = pl.program_id(0)
    n_lt = width // LANES
    x = x_ref[...]
    ms = jnp.mean(x * x, axis=-1, keepdims=True)
    h = (x * lax.rsqrt(ms + EPS) * g_ref[...]).astype(BF16)

    def mm(sec):
        return jnp.dot(h, w_ref[:, sec * width:(sec + 1) * width],
                       preferred_element_type=F32)

    def lane_tile(a, p):
        return a[:, p * LANES:(p + 1) * LANES]

    q = mm(0)
    for p in range(n_lt):
        qn = _head_norm(lane_tile(q, p), gq_ref[:, p * LANES:(p + 1) * LANES])
        q_ref[p] = (qn * (LOG2E * HEAD_DIM ** -0.5)).astype(BF16)
    k = mm(1)
    for p in range(n_lt):
        k_ref[p] = _head_norm(lane_tile(k, p), gk_ref[:, p * LANES:(p + 1) * LANES]).astype(BF16)
    v = mm(2)
    gf = mm(3)
    sgf = _silu(gf)
    for p in range(n_lt):
        v_ref[p] = lane_tile(v, p).astype(BF16)
        sgf_ref[p] = lane_tile(sgf, p).astype(BF16)
    glu_a = mm(4)
    glu_b = mm(5)
    u_ref[...] = (glu_a * _sigmoid(glu_b)).astype(BF16)
    gc = mm(6)
    sgc_ref[...] = _silu(gc).astype(BF16)

    zf = jnp.dot(h, wf_ref[...], preferred_element_type=F32) + bf_ref[...]
    log_f = jnp.minimum(zf, 0.0) - jnp.log1p(jnp.exp(-jnp.abs(zf)))
    pieces = jnp.concatenate(_split3(log_f), axis=1)
    cs = jnp.dot(tri_ref[...], pieces, preferred_element_type=F32)
    c_tile = cs[:, :LANES] + cs[:, LANES:2 * LANES] + cs[:, 2 * LANES:]

    @pl.when(i % tiles_per_seq == 0)
    def _():
        carry_ref[...] = jnp.zeros_like(carry_ref)

    c_full = c_tile + carry_ref[...]
    carry_ref[...] = c_full[c_full.shape[0] - 1:, :]

    c_pieces = jnp.concatenate(_split3(c_full * LOG2E), axis=1)
    slot = lax.broadcasted_iota(jnp.int32, (1, LANES), 1) & (DECAY_LANES - 1)
    kdec = jnp.dot(c_pieces, pk_ref[...], preferred_element_type=F32) + jnp.where(slot < 3, 1.0, 0.0)
    qdec = jnp.dot(c_pieces, pq_ref[...], preferred_element_type=F32) + jnp.where((slot >= 3) & (slot < 6), 1.0, 0.0)
    kdec_ref[...] = kdec.astype(BF16)
    qdec_ref[...] = qdec.astype(BF16)


def _decay_placement(n_heads):
    pk = np.zeros((3 * LANES, LANES), np.float32)
    pq = np.zeros((3 * LANES, LANES), np.float32)
    for head in range(n_heads):
        for piece in range(3):
            pq[piece * LANES + head, DECAY_LANES * head + piece] = 1.0
            pk[piece * LANES + head, DECAY_LANES * head + 3 + piece] = -1.0
    return jnp.asarray(pk, BF16), jnp.asarray(pq, BF16)


def _inproj(x2d, norm_g, w_main, w_f, b_f, gq, gk, n_heads, seq_len):
    m, d = x2d.shape
    tm = ROW_TILE
    width = d
    n_lt = width // LANES
    tri = (lax.broadcasted_iota(jnp.int32, (tm, tm), 0)
           >= lax.broadcasted_iota(jnp.int32, (tm, tm), 1)).astype(BF16)
    pk, pq = _decay_placement(n_heads)
    row = lambda i: (i, 0)
    const = lambda i: (0, 0)
    tiled = pl.BlockSpec((n_lt, tm, LANES), lambda i: (0, i, 0))
    out_tiled = jax.ShapeDtypeStruct((n_lt, m, LANES), BF16)
    out_bf = jax.ShapeDtypeStruct((m, width), BF16)
    out_dec = jax.ShapeDtypeStruct((m, LANES), BF16)
    return pl.pallas_call(
        functools.partial(_inproj_kernel, tiles_per_seq=seq_len // tm, width=width),
        grid=(m // tm,),
        in_specs=[
            pl.BlockSpec((tm, d), row),
            pl.BlockSpec((1, d), const),
            pl.BlockSpec(w_main.shape, const, pipeline_mode=pl.Buffered(1)),
            pl.BlockSpec(w_f.shape, const),
            pl.BlockSpec((1, LANES), const),
            pl.BlockSpec((tm, tm), const),
            pl.BlockSpec((1, width), const),
            pl.BlockSpec((1, width), const),
            pl.BlockSpec(pk.shape, const),
            pl.BlockSpec(pq.shape, const),
        ],
        out_specs=[tiled] * 4 + [pl.BlockSpec((tm, width), row)] * 2 + [pl.BlockSpec((tm, LANES), row)] * 2,
        out_shape=[out_tiled] * 4 + [out_bf] * 2 + [out_dec] * 2,
        scratch_shapes=[pltpu.VMEM((1, LANES), F32)],
        compiler_params=pltpu.CompilerParams(
            dimension_semantics=("arbitrary",), vmem_limit_bytes=VMEM_LIMIT),
        name="inproj",
    )(x2d, norm_g, w_main, w_f, b_f, tri, gq, gk, pk, pq)


def _attn_kernel(q_ref, k_ref, v_ref, kdec_ref, qdec_ref, gate_ref, o_ref,
                 qaugt_ref, vt_ref, s_ref, bias_ref, acc_ref, m_ref):
    t = ATT_TILE
    seq = q_ref.shape[1]
    n_tiles = seq // t
    pair = pl.program_id(1)
    sub = lax.broadcasted_iota(jnp.int32, (LANES, 1), 0)

    def prologue(r, _):
        rows = pl.ds(pl.multiple_of(r * t, t), t)
        q_t = q_ref[0, rows, :].astype(F32).T
        d_t = qdec_ref[rows, :].astype(F32).T
        v_t = v_ref[0, rows, :].astype(F32).T
        for h in range(HEADS_PER_STEP):
            in_head = (sub >= h * HEAD_DIM) & (sub < (h + 1) * HEAD_DIM)
            qaugt_ref[h, r, 0:LANES, :] = jnp.where(in_head, q_t, 0.0).astype(BF16)
            first_lane = (pair * HEADS_PER_STEP + h) * DECAY_LANES
            own_decay = (sub >= first_lane) & (sub < first_lane + DECAY_LANES)
            qaugt_ref[h, r, LANES:2 * LANES, :] = jnp.where(own_decay, d_t, 0.0).astype(BF16)
            vt_ref[r, h * V_ROWS:h * V_ROWS + HEAD_DIM, :] = v_t[h * HEAD_DIM:(h + 1) * HEAD_DIM, :].astype(BF16)
            vt_ref[r, h * V_ROWS + HEAD_DIM:(h + 1) * V_ROWS, :] = jnp.ones((V_ROWS - HEAD_DIM, t), BF16)
        return 0

    lax.fori_loop(0, n_tiles, prologue, 0)

    k_idx = lax.broadcasted_iota(jnp.int32, (t, t), 0)
    q_idx = lax.broadcasted_iota(jnp.int32, (t, t), 1)
    bias_ref[0] = jnp.full((t, t), jnp.inf, F32)
    bias_ref[1] = jnp.where(k_idx <= q_idx, jnp.inf, NEG_INF)
    acc_ref[...] = jnp.zeros_like(acc_ref)
    m_ref[...] = jnp.full(m_ref.shape, NEG_INF, F32)

    def qk(i, j):
        krows = pl.ds(pl.multiple_of(j * t, t), t)
        kaug = jnp.concatenate([k_ref[0, krows, :], kdec_ref[krows, :]], axis=1)
        return [jnp.dot(kaug, qaugt_ref[h, i], preferred_element_type=F32) for h in range(HEADS_PER_STEP)]

    def store_scores(slot, i, j, s):
        bias = bias_ref[(i == j).astype(jnp.int32)]
        blk_max = []
        for h in range(HEADS_PER_STEP):
            s_h = jnp.minimum(s[h], bias)
            blk_max.append(jnp.max(s_h, axis=0, keepdims=True))
            s_ref[slot, h] = s_h
        return blk_max

    def softmax_pv(slot, i, j, blk_max):
        for h in range(HEADS_PER_STEP):
            m_old = m_ref[i, h:h + 1, :]
            m_new = jnp.maximum(m_old, blk_max[h])
            p = jnp.exp2(s_ref[slot, h] - m_new)
            alpha = jnp.exp2(m_old - m_new)
            hrows = slice(h * V_ROWS, (h + 1) * V_ROWS)
            pv = jnp.dot(vt_ref[j, hrows, :], p.astype(BF16), preferred_element_type=F32)
            acc_ref[i, hrows, :] = alpha * acc_ref[i, hrows, :] + pv
            m_ref[i, h:h + 1, :] = m_new

    n_blocks = n_tiles * (n_tiles + 1) // 2
    assert n_blocks % ATT_PIPE == 0

    def advance(i, j):
        wrap = j >= i
        return jnp.where(wrap, i + 1, i), jnp.where(wrap, 0, j + 1)

    ij, maxes = [], []
    cur = (jnp.int32(0), jnp.int32(0))
    for slot in range(ATT_PIPE):
        ij.append(cur)
        maxes.append(store_scores(slot, *cur, qk(*cur)))
        cur = advance(*cur)

    def trip(_, carry):
        ij, maxes = carry
        nxt = ij[-1]
        new_ij, new_maxes = [], []
        for slot in range(ATT_PIPE):
            nxt = advance(*nxt)
            s_next = qk(jnp.minimum(nxt[0], n_tiles - 1), nxt[1])
            softmax_pv(slot, *ij[slot], maxes[slot])
            new_maxes.append(store_scores(slot, *nxt, s_next))
            new_ij.append(nxt)
        return new_ij, new_maxes

    lax.fori_loop(0, n_blocks // ATT_PIPE, trip, (ij, maxes))

    def finalize(i, _):
        parts = []
        for h in range(HEADS_PER_STEP):
            row_sum = acc_ref[i, h * V_ROWS + HEAD_DIM:h * V_ROWS + HEAD_DIM + 1, :]
            parts.append(acc_ref[i, h * V_ROWS:h * V_ROWS + HEAD_DIM, :] * (1.0 / row_sum))
        rows = pl.ds(pl.multiple_of(i * t, t), t)
        out_t = jnp.concatenate(parts, axis=0)
        o_ref[0, rows, :] = (out_t.T * gate_ref[0, rows, :].astype(F32)).astype(BF16)
        return 0

    lax.fori_loop(0, n_tiles, finalize, 0)


def _attention(q, k, v, kdec, qdec, gate, batch, seq_len):
    n_pairs, m, _ = q.shape
    t = ATT_TILE
    slab = pl.BlockSpec((1, seq_len, LANES), lambda b, p: (p, b, 0))
    shared = pl.BlockSpec((seq_len, LANES), lambda b, p: (b, 0))
    return pl.pallas_call(
        _attn_kernel,
        grid=(batch, n_pairs),
        in_specs=[slab, slab, slab, shared, shared, slab],
        out_specs=slab,
        out_shape=jax.ShapeDtypeStruct((n_pairs, m, LANES), BF16),
        scratch_shapes=[
            pltpu.VMEM((HEADS_PER_STEP, seq_len // t, 2 * LANES, t), BF16),
            pltpu.VMEM((seq_len // t, HEADS_PER_STEP * V_ROWS, t), BF16),
            pltpu.VMEM((ATT_PIPE, HEADS_PER_STEP, t, t), F32),
            pltpu.VMEM((2, t, t), F32),
            pltpu.VMEM((seq_len // t, HEADS_PER_STEP * V_ROWS, t), F32),
            pltpu.VMEM((seq_len // t, HEADS_PER_STEP, t), F32),
        ],
        compiler_params=pltpu.CompilerParams(
            dimension_semantics=("arbitrary", "arbitrary"), vmem_limit_bytes=VMEM_LIMIT),
        name="fox_attention",
    )(q, k, v, kdec, qdec, gate)


def _conv_kernel(u_ref, gate_ref, w_ref, b_ref, lng_ref, lnb_ref, o_ref, ext_ref, sh_ref, y_ref):
    ts = u_ref.shape[0]
    n_lt = ext_ref.shape[0]
    s_idx = pl.program_id(1)

    @pl.when(s_idx == 0)
    def _():
        ext_ref[:, 0:CONV_HALO, :] = jnp.zeros((n_lt, CONV_HALO, LANES), F32)

    @pl.when(s_idx != 0)
    def _():
        ext_ref[:, 0:CONV_HALO, :] = ext_ref[:, ts:ts + CONV_HALO, :]

    for lt in range(n_lt):
        ext_ref[lt, CONV_HALO:CONV_HALO + ts, :] = u_ref[:, lt * LANES:(lt + 1) * LANES].astype(F32)

    rc = CONV_CHUNK
    first = CONV_HALO - (CONV_KERNEL - 1)
    sh_rows = sh_ref.shape[1]

    def lane_tile(lt, _):
        for b in range(1, SUBLANES):
            sh_ref[b - 1] = ext_ref[lt, b:b + sh_rows, :]

        def chunk(c, _):
            r0 = pl.multiple_of(c * rc, rc)
            acc = jnp.zeros((rc, LANES), F32) + b_ref[lt]
            for j in range(CONV_KERNEL):
                a, b = divmod(first + j, SUBLANES)
                rows = pl.ds(r0 + a * SUBLANES, rc)
                tap = ext_ref[lt, rows, :] if b == 0 else sh_ref[b - 1, rows, :]
                acc = acc + tap * w_ref[lt, j:j + 1, :]
            y_ref[lt, pl.ds(r0, rc), :] = acc
            return 0

        lax.fori_loop(0, ts // rc, chunk, 0)
        return 0

    lax.fori_loop(0, n_lt, lane_tile, 0)

    y = jnp.concatenate([y_ref[lt] for lt in range(n_lt)], axis=1)
    mu = jnp.mean(y, axis=-1, keepdims=True)
    yc = y - mu
    var = jnp.mean(yc * yc, axis=-1, keepdims=True)
    z = yc * lax.rsqrt(var + EPS) * lng_ref[...] + lnb_ref[...]
    o_ref[...] = (_silu(z) * gate_ref[...].astype(F32)).astype(BF16)


def _conv_branch(u, gate, conv_w, conv_b, ln_g, ln_b, batch, seq_len):
    m, width = u.shape
    ts = CONV_TILE
    n_s = seq_len // ts
    n_lt = width // LANES
    w_tiles = conv_w.reshape(CONV_KERNEL, n_lt, LANES).transpose(1, 0, 2)
    b_tiles = conv_b.reshape(n_lt, 1, LANES)
    row = lambda b, s: (b * n_s + s, 0)
    const2 = lambda b, s: (0, 0)
    const3 = lambda b, s: (0, 0, 0)
    return pl.pallas_call(
        _conv_kernel,
        grid=(batch, n_s),
        in_specs=[pl.BlockSpec((ts, width), row), pl.BlockSpec((ts, width), row),
                  pl.BlockSpec(w_tiles.shape, const3), pl.BlockSpec(b_tiles.shape, const3),
                  pl.BlockSpec((1, width), const2), pl.BlockSpec((1, width), const2)],
        out_specs=pl.BlockSpec((ts, width), row),
        out_shape=jax.ShapeDtypeStruct((m, width), BF16),
        scratch_shapes=[pltpu.VMEM((n_lt, CONV_HALO + ts, LANES), F32),
                        pltpu.VMEM((SUBLANES - 1, CONV_HALO + ts - SUBLANES, LANES), F32),
                        pltpu.VMEM((n_lt, ts, LANES), F32)],
        compiler_params=pltpu.CompilerParams(
            dimension_semantics=("arbitrary", "arbitrary"), vmem_limit_bytes=VMEM_LIMIT),
        name="conv_branch",
    )(u, gate, w_tiles, b_tiles, ln_g, ln_b)


def _outproj_kernel(x_ref, a_ref, u_ref, w_ref, o_ref):
    a = jnp.concatenate([a_ref[p] for p in range(a_ref.shape[0])], axis=1)
    half = a.shape[1]
    o_ref[...] = (x_ref[...]
                  + jnp.dot(a, w_ref[0:half, :], preferred_element_type=F32)
                  + jnp.dot(u_ref[...], w_ref[half:, :], preferred_element_type=F32))


def _outproj(x2d, a, u, w_out):
    m, d = x2d.shape
    tm = ROW_TILE
    row = lambda i: (i, 0)
    return pl.pallas_call(
        _outproj_kernel,
        grid=(m // tm,),
        in_specs=[pl.BlockSpec((tm, d), row), pl.BlockSpec((a.shape[0], tm, LANES), lambda i: (0, i, 0)),
                  pl.BlockSpec((tm, u.shape[1]), row),
                  pl.BlockSpec(w_out.shape, lambda i: (0, 0))],
        out_specs=pl.BlockSpec((tm, d), row),
        out_shape=jax.ShapeDtypeStruct((m, d), F32),
        compiler_params=pltpu.CompilerParams(
            dimension_semantics=("arbitrary",), vmem_limit_bytes=VMEM_LIMIT),
        name="outproj",
    )(x2d, a, u, w_out)


def kernel(x, norm_g, w_in, b_forget, q_norm_g, k_norm_g, conv_w, conv_b, conv_ln_g, conv_ln_b, w_out):
    batch, seq_len, d_model = x.shape
    depth = w_in.shape[0]
    n_heads = b_forget.shape[1]
    fox_width = n_heads * HEAD_DIM
    conv_width = conv_w.shape[2]
    assert fox_width == d_model and conv_width == d_model
    assert seq_len % ROW_TILE == 0 and seq_len % ATT_TILE == 0 and seq_len % CONV_TILE == 0
    assert n_heads * DECAY_LANES <= LANES and CONV_HALO >= CONV_KERNEL - 1

    o_f = 3 * fox_width
    o_gf = o_f + n_heads
    x2d = x.reshape(batch * seq_len, d_model)
    for l in range(depth):
        w = w_in[l]
        w_main = jnp.concatenate([w[:, :o_f], w[:, o_gf:]], axis=1).astype(BF16)
        w_f = jnp.pad(w[:, o_f:o_gf], ((0, 0), (0, LANES - n_heads))).astype(BF16)
        b_f = jnp.pad(b_forget[l], (0, LANES - n_heads)).reshape(1, LANES)
        q, k, v, sgf, u, sgc, kdec, qdec = _inproj(
            x2d, norm_g[l].reshape(1, d_model), w_main, w_f, b_f,
            q_norm_g[l].reshape(1, fox_width), k_norm_g[l].reshape(1, fox_width), n_heads, seq_len)
        a = _attention(q, k, v, kdec, qdec, sgf, batch, seq_len)
        yu = _conv_branch(u, sgc, conv_w[l], conv_b[l],
                          conv_ln_g[l].reshape(1, conv_width), conv_ln_b[l].reshape(1, conv_width),
                          batch, seq_len)
        x2d = _outproj(x2d, a, yu, w_out[l].astype(BF16))
    return x2d.reshape(batch, seq_len, d_model)
```

```python
import functools
import math

import jax
import jax.numpy as jnp
import numpy as np
from jax import lax
from jax.experimental import pallas as pl
from jax.experimental.pallas import tpu as pltpu

F32 = jnp.float32
BF16 = jnp.bfloat16

HEAD_DIM = 64
CONV_KERNEL = 31
EPS = 1e-6
NEG_INF = -1e30
LOG2E = math.log2(math.e)

LANES = 128
SUBLANES = 8
BF16_ROWS = 16
HEADS_PER_STEP = LANES // HEAD_DIM
DECAY_LANES = 8
VMEM_LIMIT = 56 * 1024 * 1024

ROW_TILE = 512
ATT_TILE = 256
ATT_PIPE = 6
V_ROWS = HEAD_DIM + BF16_ROWS
CONV_TILE = 256
CONV_HALO = 32
CONV_CHUNK = 128


def _split3(x):
    hi = x.astype(BF16)
    r1 = x - hi.astype(F32)
    mid = r1.astype(BF16)
    lo = (r1 - mid.astype(F32)).astype(BF16)
    return hi, mid, lo


def _sigmoid(x):
    return 0.5 * jnp.tanh(0.5 * x) + 0.5


def _silu(x):
    h = 0.5 * x
    return h + h * jnp.tanh(h)


def _head_norm(xf, gain):
    lo_half = lax.broadcasted_iota(jnp.int32, (1, LANES), 1) < HEAD_DIM
    sq = xf * xf
    s_lo = jnp.sum(jnp.where(lo_half, sq, 0.0), axis=-1, keepdims=True)
    s_hi = jnp.sum(jnp.where(lo_half, 0.0, sq), axis=-1, keepdims=True)
    mean_sq = jnp.where(lo_half, s_lo, s_hi) * (1.0 / HEAD_DIM)
    return xf * lax.rsqrt(mean_sq + EPS) * gain


def _inproj_kernel(x_ref, g_ref, w_ref, wf_ref, bf_ref, tri_ref, gq_ref, gk_ref, pk_ref, pq_ref,
                   q_ref, k_ref, v_ref, sgf_ref, u_ref, sgc_ref, kdec_ref, qdec_ref,
                   carry_ref, *, tiles_per_seq, width):
    i = pl.program_id(0)
    n_lt = width // LANES
    x = x_ref[...]
    ms = jnp.mean(x * x, axis=-1, keepdims=True)
    h = (x * lax.rsqrt(ms + EPS) * g_ref[...]).astype(BF16)

    def mm(sec):
        return jnp.dot(h, w_ref[:, sec * width:(sec + 1) * width],
                       preferred_element_type=F32)

    def lane_tile(a, p):
        return a[:, p * LANES:(p + 1) * LANES]

    q = mm(0)
    for p in range(n_lt):
        qn = _head_norm(lane_tile(q, p), gq_ref[:, p * LANES:(p + 1) * LANES])
        q_ref[p] = (qn * (LOG2E * HEAD_DIM ** -0.5)).astype(BF16)
    k = mm(1)
    for p in range(n_lt):
        k_ref[p] = _head_norm(lane_tile(k, p), gk_ref[:, p * LANES:(p + 1) * LANES]).astype(BF16)
    v = mm(2)
    gf = mm(3)
    sgf = _silu(gf)
    for p in range(n_lt):
        v_ref[p] = lane_tile(v, p).astype(BF16)
        sgf_ref[p] = lane_tile(sgf, p).astype(BF16)
    glu_a = mm(4)
    glu_b = mm(5)
    u_ref[...] = (glu_a * _sigmoid(glu_b)).astype(BF16)
    gc = mm(6)
    sgc_ref[...] = _silu(gc).astype(BF16)

    zf = jnp.dot(h, wf_ref[...], preferred_element_type=F32) + bf_ref[...]
    log_f = jnp.minimum(zf, 0.0) - jnp.log1p(jnp.exp(-jnp.abs(zf)))
    pieces = jnp.concatenate(_split3(log_f), axis=1)
    cs = jnp.dot(tri_ref[...], pieces, preferred_element_type=F32)
    c_tile = cs[:, :LANES] + cs[:, LANES:2 * LANES] + cs[:, 2 * LANES:]

    @pl.when(i % tiles_per_seq == 0)
    def _():
        carry_ref[...] = jnp.zeros_like(carry_ref)

    c_full = c_tile + carry_ref[...]
    carry_ref[...] = c_full[c_full.shape[0] - 1:, :]

    c_pieces = jnp.concatenate(_split3(c_full * LOG2E), axis=1)
    slot = lax.broadcasted_iota(jnp.int32, (1, LANES), 1) & (DECAY_LANES - 1)
    kdec = jnp.dot(c_pieces, pk_ref[...], preferred_element_type=F32) + jnp.where(slot < 3, 1.0, 0.0)
    qdec = jnp.dot(c_pieces, pq_ref[...], preferred_element_type=F32) + jnp.where((slot >= 3) & (slot < 6), 1.0, 0.0)
    kdec_ref[...] = kdec.astype(BF16)
    qdec_ref[...] = qdec.astype(BF16)


def _decay_placement(n_heads):
    pk = np.zeros((3 * LANES, LANES), np.float32)
    pq = np.zeros((3 * LANES, LANES), np.float32)
    for head in range(n_heads):
        for piece in range(3):
            pq[piece * LANES + head, DECAY_LANES * head + piece] = 1.0
            pk[piece * LANES + head, DECAY_LANES * head + 3 + piece] = -1.0
    return jnp.asarray(pk, BF16), jnp.asarray(pq, BF16)


def _inproj(x2d, norm_g, w_main, w_f, b_f, gq, gk, n_heads, seq_len):
    m, d = x2d.shape
    tm = ROW_TILE
    width = d
    n_lt = width // LANES
    tri = (lax.broadcasted_iota(jnp.int32, (tm, tm), 0)
           >= lax.broadcasted_iota(jnp.int32, (tm, tm), 1)).astype(BF16)
    pk, pq = _decay_placement(n_heads)
    row = lambda i: (i, 0)
    const = lambda i: (0, 0)
    tiled = pl.BlockSpec((n_lt, tm, LANES), lambda i: (0, i, 0))
    out_tiled = jax.ShapeDtypeStruct((n_lt, m, LANES), BF16)
    out_bf = jax.ShapeDtypeStruct((m, width), BF16)
    out_dec = jax.ShapeDtypeStruct((m, LANES), BF16)
    return pl.pallas_call(
        functools.partial(_inproj_kernel, tiles_per_seq=seq_len // tm, width=width),
        grid=(m // tm,),
        in_specs=[
            pl.BlockSpec((tm, d), row),
            pl.BlockSpec((1, d), const),
            pl.BlockSpec(w_main.shape, const, pipeline_mode=pl.Buffered(1)),
            pl.BlockSpec(w_f.shape, const),
            pl.BlockSpec((1, LANES), const),
            pl.BlockSpec((tm, tm), const),
            pl.BlockSpec((1, width), const),
            pl.BlockSpec((1, width), const),
            pl.BlockSpec(pk.shape, const),
            pl.BlockSpec(pq.shape, const),
        ],
        out_specs=[tiled] * 4 + [pl.BlockSpec((tm, width), row)] * 2 + [pl.BlockSpec((tm, LANES), row)] * 2,
        out_shape=[out_tiled] * 4 + [out_bf] * 2 + [out_dec] * 2,
        scratch_shapes=[pltpu.VMEM((1, LANES), F32)],
        compiler_params=pltpu.CompilerParams(
            dimension_semantics=("arbitrary",), vmem_limit_bytes=VMEM_LIMIT),
        name="inproj",
    )(x2d, norm_g, w_main, w_f, b_f, tri, gq, gk, pk, pq)


def _attn_kernel(q_ref, k_ref, v_ref, kdec_ref, qdec_ref, gate_ref, o_ref,
                 qaugt_ref, vt_ref, s_ref, bias_ref, acc_ref, m_ref):
    t = ATT_TILE
    seq = q_ref.shape[1]
    n_tiles = seq // t
    pair = pl.program_id(1)
    sub = lax.broadcasted_iota(jnp.int32, (LANES, 1), 0)

    def prologue(r, _):
        rows = pl.ds(pl.multiple_of(r * t, t), t)
        q_t = q_ref[0, rows, :].astype(F32).T
        d_t = qdec_ref[rows, :].astype(F32).T
        v_t = v_ref[0, rows, :].astype(F32).T
        for h in range(HEADS_PER_STEP):
            in_head = (sub >= h * HEAD_DIM) & (sub < (h + 1) * HEAD_DIM)
            qaugt_ref[h, r, 0:LANES, :] = jnp.where(in_head, q_t, 0.0).astype(BF16)
            first_lane = (pair * HEADS_PER_STEP + h) * DECAY_LANES
            own_decay = (sub >= first_lane) & (sub < first_lane + DECAY_LANES)
            qaugt_ref[h, r, LANES:2 * LANES, :] = jnp.where(own_decay, d_t, 0.0).astype(BF16)
            vt_ref[r, h * V_ROWS:h * V_ROWS + HEAD_DIM, :] = v_t[h * HEAD_DIM:(h + 1) * HEAD_DIM, :].astype(BF16)
            vt_ref[r, h * V_ROWS + HEAD_DIM:(h + 1) * V_ROWS, :] = jnp.ones((V_ROWS - HEAD_DIM, t), BF16)
        return 0

    lax.fori_loop(0, n_tiles, prologue, 0)

    k_idx = lax.broadcasted_iota(jnp.int32, (t, t), 0)
    q_idx = lax.broadcasted_iota(jnp.int32, (t, t), 1)
    bias_ref[0] = jnp.full((t, t), jnp.inf, F32)
    bias_ref[1] = jnp.where(k_idx <= q_idx, jnp.inf, NEG_INF)
    acc_ref[...] = jnp.zeros_like(acc_ref)
    m_ref[...] = jnp.full(m_ref.shape, NEG_INF, F32)

    def qk(i, j):
        krows = pl.ds(pl.multiple_of(j * t, t), t)
        kaug = jnp.concatenate([k_ref[0, krows, :], kdec_ref[krows, :]], axis=1)
        return [jnp.dot(kaug, qaugt_ref[h, i], preferred_element_type=F32) for h in range(HEADS_PER_STEP)]

    def store_scores(slot, i, j, s):
        bias = bias_ref[(i == j).astype(jnp.int32)]
        blk_max = []
        for h in range(HEADS_PER_STEP):
            s_h = jnp.minimum(s[h], bias)
            blk_max.append(jnp.max(s_h, axis=0, keepdims=True))
            s_ref[slot, h] = s_h
        return blk_max

    def softmax_pv(slot, i, j, blk_max):
        for h in range(HEADS_PER_STEP):
            m_old = m_ref[i, h:h + 1, :]
            m_new = jnp.maximum(m_old, blk_max[h])
            p = jnp.exp2(s_ref[slot, h] - m_new)
            alpha = jnp.exp2(m_old - m_new)
            hrows = slice(h * V_ROWS, (h + 1) * V_ROWS)
            pv = jnp.dot(vt_ref[j, hrows, :], p.astype(BF16), preferred_element_type=F32)
            acc_ref[i, hrows, :] = alpha * acc_ref[i, hrows, :] + pv
            m_ref[i, h:h + 1, :] = m_new

    n_blocks = n_tiles * (n_tiles + 1) // 2
    assert n_blocks % ATT_PIPE == 0

    def advance(i, j):
        wrap = j >= i
        return jnp.where(wrap, i + 1, i), jnp.where(wrap, 0, j + 1)

    ij, maxes = [], []
    cur = (jnp.int32(0), jnp.int32(0))
    for slot in range(ATT_PIPE):
        ij.append(cur)
        maxes.append(store_scores(slot, *cur, qk(*cur)))
        cur = advance(*cur)

    def trip(_, carry):
        ij, maxes = carry
        nxt = ij[-1]
        new_ij, new_maxes = [], []
        for slot in range(ATT_PIPE):
            nxt = advance(*nxt)
            s_next = qk(jnp.minimum(nxt[0], n_tiles - 1), nxt[1])
            softmax_pv(slot, *ij[slot], maxes[slot])
            new_maxes.append(store_scores(slot, *nxt, s_next))
            new_ij.append(nxt)
        return new_ij, new_maxes

    lax.fori_loop(0, n_blocks // ATT_PIPE, trip, (ij, maxes))

    def finalize(i, _):
        parts = []
        for h in range(HEADS_PER_STEP):
            row_sum = acc_ref[i, h * V_ROWS + HEAD_DIM:h * V_ROWS + HEAD_DIM + 1, :]
            parts.append(acc_ref[i, h * V_ROWS:h * V_ROWS + HEAD_DIM, :] * (1.0 / row_sum))
        rows = pl.ds(pl.multiple_of(i * t, t), t)
        out_t = jnp.concatenate(parts, axis=0)
        o_ref[0, rows, :] = (out_t.T * gate_ref[0, rows, :].astype(F32)).astype(BF16)
        return 0

    lax.fori_loop(0, n_tiles, finalize, 0)


def _attention(q, k, v, kdec, qdec, gate, batch, seq_len):
    n_pairs, m, _ = q.shape
    t = ATT_TILE
    slab = pl.BlockSpec((1, seq_len, LANES), lambda b, p: (p, b, 0))
    shared = pl.BlockSpec((seq_len, LANES), lambda b, p: (b, 0))
    return pl.pallas_call(
        _attn_kernel,
        grid=(batch, n_pairs),
        in_specs=[slab, slab, slab, shared, shared, slab],
        out_specs=slab,
        out_shape=jax.ShapeDtypeStruct((n_pairs, m, LANES), BF16),
        scratch_shapes=[
            pltpu.VMEM((HEADS_PER_STEP, seq_len // t, 2 * LANES, t), BF16),
            pltpu.VMEM((seq_len // t, HEADS_PER_STEP * V_ROWS, t), BF16),
            pltpu.VMEM((ATT_PIPE, HEADS_PER_STEP, t, t), F32),
            pltpu.VMEM((2, t, t), F32),
            pltpu.VMEM((seq_len // t, HEADS_PER_STEP * V_ROWS, t), F32),
            pltpu.VMEM((seq_len // t, HEADS_PER_STEP, t), F32),
        ],
        compiler_params=pltpu.CompilerParams(
            dimension_semantics=("arbitrary", "arbitrary"), vmem_limit_bytes=VMEM_LIMIT),
        name="fox_attention",
    )(q, k, v, kdec, qdec, gate)


def _conv_out_kernel(u_ref, gate_ref, w_ref, b_ref, lng_ref, lnb_ref, x_ref, a_ref, wout_ref, o_ref,
                     ext_ref, sh_ref, y_ref):
    ts = u_ref.shape[0]
    n_lt = ext_ref.shape[0]
    s_idx = pl.program_id(1)

    @pl.when(s_idx == 0)
    def _():
        ext_ref[:, 0:CONV_HALO, :] = jnp.zeros((n_lt, CONV_HALO, LANES), F32)

    @pl.when(s_idx != 0)
    def _():
        ext_ref[:, 0:CONV_HALO, :] = ext_ref[:, ts:ts + CONV_HALO, :]

    for lt in range(n_lt):
        ext_ref[lt, CONV_HALO:CONV_HALO + ts, :] = u_ref[:, lt * LANES:(lt + 1) * LANES].astype(F32)

    rc = CONV_CHUNK
    first = CONV_HALO - (CONV_KERNEL - 1)
    sh_rows = sh_ref.shape[1]

    def lane_tile(lt, _):
        for b in range(1, SUBLANES):
            sh_ref[b - 1] = ext_ref[lt, b:b + sh_rows, :]

        def chunk(c, _):
            r0 = pl.multiple_of(c * rc, rc)
            acc = jnp.zeros((rc, LANES), F32) + b_ref[lt]
            for j in range(CONV_KERNEL):
                a, b = divmod(first + j, SUBLANES)
                rows = pl.ds(r0 + a * SUBLANES, rc)
                tap = ext_ref[lt, rows, :] if b == 0 else sh_ref[b - 1, rows, :]
                acc = acc + tap * w_ref[lt, j:j + 1, :]
            y_ref[lt, pl.ds(r0, rc), :] = acc
            return 0

        lax.fori_loop(0, ts // rc, chunk, 0)
        return 0

    lax.fori_loop(0, n_lt, lane_tile, 0)

    y = jnp.concatenate([y_ref[lt] for lt in range(n_lt)], axis=1)
    mu = jnp.mean(y, axis=-1, keepdims=True)
    yc = y - mu
    var = jnp.mean(yc * yc, axis=-1, keepdims=True)
    z = yc * lax.rsqrt(var + EPS) * lng_ref[...] + lnb_ref[...]
    yu = (_silu(z) * gate_ref[...].astype(F32)).astype(BF16)
    a = jnp.concatenate([a_ref[p] for p in range(a_ref.shape[0])], axis=1)
    half = a.shape[1]
    o_ref[...] = (x_ref[...]
                  + jnp.dot(a, wout_ref[0:half, :], preferred_element_type=F32)
                  + jnp.dot(yu, wout_ref[half:, :], preferred_element_type=F32))


def _conv_outproj(u, gate, conv_w, conv_b, ln_g, ln_b, x2d, a, w_out, batch, seq_len):
    m, width = u.shape
    d = x2d.shape[1]
    ts = CONV_TILE
    n_s = seq_len // ts
    n_lt = width // LANES
    w_tiles = conv_w.reshape(CONV_KERNEL, n_lt, LANES).transpose(1, 0, 2)
    b_tiles = conv_b.reshape(n_lt, 1, LANES)
    row = lambda b, s: (b * n_s + s, 0)
    const2 = lambda b, s: (0, 0)
    const3 = lambda b, s: (0, 0, 0)
    return pl.pallas_call(
        _conv_out_kernel,
        grid=(batch, n_s),
        in_specs=[pl.BlockSpec((ts, width), row), pl.BlockSpec((ts, width), row),
                  pl.BlockSpec(w_tiles.shape, const3), pl.BlockSpec(b_tiles.shape, const3),
                  pl.BlockSpec((1, width), const2), pl.BlockSpec((1, width), const2),
                  pl.BlockSpec((ts, d), row),
                  pl.BlockSpec((a.shape[0], ts, LANES), lambda b, s: (0, b * n_s + s, 0)),
                  pl.BlockSpec(w_out.shape, const2)],
        out_specs=pl.BlockSpec((ts, d), row),
        out_shape=jax.ShapeDtypeStruct((m, d), F32),
        scratch_shapes=[pltpu.VMEM((n_lt, CONV_HALO + ts, LANES), F32),
                        pltpu.VMEM((SUBLANES - 1, CONV_HALO + ts - SUBLANES, LANES), F32),
                        pltpu.VMEM((n_lt, ts, LANES), F32)],
        compiler_params=pltpu.CompilerParams(
            dimension_semantics=("arbitrary", "arbitrary"), vmem_limit_bytes=VMEM_LIMIT),
        name="conv_outproj",
    )(u, gate, w_tiles, b_tiles, ln_g, ln_b, x2d, a, w_out)


def kernel(x, norm_g, w_in, b_forget, q_norm_g, k_norm_g, conv_w, conv_b, conv_ln_g, conv_ln_b, w_out):
    batch, seq_len, d_model = x.shape
    depth = w_in.shape[0]
    n_heads = b_forget.shape[1]
    fox_width = n_heads * HEAD_DIM
    conv_width = conv_w.shape[2]
    assert fox_width == d_model and conv_width == d_model
    assert seq_len % ROW_TILE == 0 and seq_len % ATT_TILE == 0 and seq_len % CONV_TILE == 0
    assert n_heads * DECAY_LANES <= LANES and CONV_HALO >= CONV_KERNEL - 1

    o_f = 3 * fox_width
    o_gf = o_f + n_heads
    x2d = x.reshape(batch * seq_len, d_model)
    for l in range(depth):
        w = w_in[l].astype(BF16)
        w_main = jnp.concatenate([w[:, :o_f], w[:, o_gf:]], axis=1)
        w_f = jnp.pad(w[:, o_f:o_gf], ((0, 0), (0, LANES - n_heads)))
        b_f = jnp.pad(b_forget[l], (0, LANES - n_heads)).reshape(1, LANES)
        q, k, v, sgf, u, sgc, kdec, qdec = _inproj(
            x2d, norm_g[l].reshape(1, d_model), w_main, w_f, b_f,
            q_norm_g[l].reshape(1, fox_width), k_norm_g[l].reshape(1, fox_width), n_heads, seq_len)
        a = _attention(q, k, v, kdec, qdec, sgf, batch, seq_len)
        x2d = _conv_outproj(u, sgc, conv_w[l], conv_b[l],
                            conv_ln_g[l].reshape(1, conv_width), conv_ln_b[l].reshape(1, conv_width),
                            x2d, a, w_out[l].astype(BF16), batch, seq_len)
    return x2d.reshape(batch, seq_len, d_model)
```

```python
import functools
import math

import jax
import jax.numpy as jnp
import numpy as np
from jax import lax
from jax.experimental import pallas as pl
from jax.experimental.pallas import tpu as pltpu

F32 = jnp.float32
BF16 = jnp.bfloat16

HEAD_DIM = 64
CONV_KERNEL = 31
EPS = 1e-6
NEG_INF = -1e30
LOG2E = math.log2(math.e)

LANES = 128
SUBLANES = 8
BF16_ROWS = 16
HEADS_PER_STEP = LANES // HEAD_DIM
DECAY_LANES = 8
VMEM_LIMIT = 56 * 1024 * 1024

ROW_TILE = 512
ATT_TILE = 256
ATT_PIPE = 2
V_ROWS = HEAD_DIM + BF16_ROWS
CONV_TILE = 256
CONV_HALO = 32
CONV_CHUNK = 128


def _split3(x):
    hi = x.astype(BF16)
    r1 = x - hi.astype(F32)
    mid = r1.astype(BF16)
    lo = (r1 - mid.astype(F32)).astype(BF16)
    return hi, mid, lo


def _sigmoid(x):
    return 0.5 * jnp.tanh(0.5 * x) + 0.5


def _silu(x):
    h = 0.5 * x
    return h + h * jnp.tanh(h)


def _head_norm(xf, gain):
    lo_half = lax.broadcasted_iota(jnp.int32, (1, LANES), 1) < HEAD_DIM
    sq = xf * xf
    s_lo = jnp.sum(jnp.where(lo_half, sq, 0.0), axis=-1, keepdims=True)
    s_hi = jnp.sum(jnp.where(lo_half, 0.0, sq), axis=-1, keepdims=True)
    mean_sq = jnp.where(lo_half, s_lo, s_hi) * (1.0 / HEAD_DIM)
    return xf * lax.rsqrt(mean_sq + EPS) * gain


def _inproj_kernel(x_ref, g_ref, w_ref, wf_ref, bf_ref, tri_ref, gq_ref, gk_ref, pk_ref, pq_ref,
                   q_ref, k_ref, v_ref, sgf_ref, u_ref, sgc_ref, kdec_ref, qdec_ref,
                   carry_ref, *, tiles_per_seq, width):
    i = pl.program_id(0)
    n_lt = width // LANES
    x = x_ref[...]
    ms = jnp.mean(x * x, axis=-1, keepdims=True)
    h = (x * lax.rsqrt(ms + EPS) * g_ref[...]).astype(BF16)

    def mm(sec):
        return jnp.dot(h, w_ref[:, sec * width:(sec + 1) * width],
                       preferred_element_type=F32)

    def lane_tile(a, p):
        return a[:, p * LANES:(p + 1) * LANES]

    q = mm(0)
    for p in range(n_lt):
        qn = _head_norm(lane_tile(q, p), gq_ref[:, p * LANES:(p + 1) * LANES])
        q_ref[p] = (qn * (LOG2E * HEAD_DIM ** -0.5)).astype(BF16)
    k = mm(1)
    for p in range(n_lt):
        k_ref[p] = _head_norm(lane_tile(k, p), gk_ref[:, p * LANES:(p + 1) * LANES]).astype(BF16)
    v = mm(2)
    gf = mm(3)
    sgf = _silu(gf)
    for p in range(n_lt):
        v_ref[p] = lane_tile(v, p).astype(BF16)
        sgf_ref[p] = lane_tile(sgf, p).astype(BF16)
    glu_a = mm(4)
    glu_b = mm(5)
    u_ref[...] = (glu_a * _sigmoid(glu_b)).astype(BF16)
    gc = mm(6)
    sgc_ref[...] = _silu(gc).astype(BF16)

    zf = jnp.dot(h, wf_ref[...], preferred_element_type=F32) + bf_ref[...]
    log_f = jnp.minimum(zf, 0.0) - jnp.log1p(jnp.exp(-jnp.abs(zf)))
    pieces = jnp.concatenate(_split3(log_f), axis=1)
    cs = jnp.dot(tri_ref[...], pieces, preferred_element_type=F32)
    c_tile = cs[:, :LANES] + cs[:, LANES:2 * LANES] + cs[:, 2 * LANES:]

    @pl.when(i % tiles_per_seq == 0)
    def _():
        carry_ref[...] = jnp.zeros_like(carry_ref)

    c_full = c_tile + carry_ref[...]
    carry_ref[...] = c_full[c_full.shape[0] - 1:, :]

    c_pieces = jnp.concatenate(_split3(c_full * LOG2E), axis=1)
    slot = lax.broadcasted_iota(jnp.int32, (1, LANES), 1) & (DECAY_LANES - 1)
    kdec = jnp.dot(c_pieces, pk_ref[...], preferred_element_type=F32) + jnp.where(slot < 3, 1.0, 0.0)
    qdec = jnp.dot(c_pieces, pq_ref[...], preferred_element_type=F32) + jnp.where((slot >= 3) & (slot < 6), 1.0, 0.0)
    kdec_ref[...] = kdec.astype(BF16)
    qdec_ref[...] = qdec.astype(BF16)


def _decay_placement(n_heads):
    pk = np.zeros((3 * LANES, LANES), np.float32)
    pq = np.zeros((3 * LANES, LANES), np.float32)
    for head in range(n_heads):
        for piece in range(3):
            pq[piece * LANES + head, DECAY_LANES * head + piece] = 1.0
            pk[piece * LANES + head, DECAY_LANES * head + 3 + piece] = -1.0
    return jnp.asarray(pk, BF16), jnp.asarray(pq, BF16)


def _inproj(x2d, norm_g, w_main, w_f, b_f, gq, gk, n_heads, seq_len):
    m, d = x2d.shape
    tm = ROW_TILE
    width = d
    n_lt = width // LANES
    tri = (lax.broadcasted_iota(jnp.int32, (tm, tm), 0)
           >= lax.broadcasted_iota(jnp.int32, (tm, tm), 1)).astype(BF16)
    pk, pq = _decay_placement(n_heads)
    row = lambda i: (i, 0)
    const = lambda i: (0, 0)
    tiled = pl.BlockSpec((n_lt, tm, LANES), lambda i: (0, i, 0))
    out_tiled = jax.ShapeDtypeStruct((n_lt, m, LANES), BF16)
    out_bf = jax.ShapeDtypeStruct((m, width), BF16)
    out_dec = jax.ShapeDtypeStruct((m, LANES), BF16)
    return pl.pallas_call(
        functools.partial(_inproj_kernel, tiles_per_seq=seq_len // tm, width=width),
        grid=(m // tm,),
        in_specs=[
            pl.BlockSpec((tm, d), row),
            pl.BlockSpec((1, d), const),
            pl.BlockSpec(w_main.shape, const, pipeline_mode=pl.Buffered(1)),
            pl.BlockSpec(w_f.shape, const),
            pl.BlockSpec((1, LANES), const),
            pl.BlockSpec((tm, tm), const),
            pl.BlockSpec((1, width), const),
            pl.BlockSpec((1, width), const),
            pl.BlockSpec(pk.shape, const),
            pl.BlockSpec(pq.shape, const),
        ],
        out_specs=[tiled] * 4 + [pl.BlockSpec((tm, width), row)] * 2 + [pl.BlockSpec((tm, LANES), row)] * 2,
        out_shape=[out_tiled] * 4 + [out_bf] * 2 + [out_dec] * 2,
        scratch_shapes=[pltpu.VMEM((1, LANES), F32)],
        compiler_params=pltpu.CompilerParams(
            dimension_semantics=("arbitrary",), vmem_limit_bytes=VMEM_LIMIT),
        name="inproj",
    )(x2d, norm_g, w_main, w_f, b_f, tri, gq, gk, pk, pq)


def _attn_kernel(q_ref, k_ref, v_ref, kdec_ref, qdec_ref, gate_ref, o_ref,
                 qaugt_ref, vt_ref, s_ref, bias_ref, acc_ref, m_ref):
    t = ATT_TILE
    seq = q_ref.shape[1]
    n_tiles = seq // t
    pair = pl.program_id(1)
    sub = lax.broadcasted_iota(jnp.int32, (LANES, 1), 0)

    def prologue(r, _):
        rows = pl.ds(pl.multiple_of(r * t, t), t)
        q_t = q_ref[0, rows, :].astype(F32).T
        d_t = qdec_ref[rows, :].astype(F32).T
        v_t = v_ref[0, rows, :].astype(F32).T
        for h in range(HEADS_PER_STEP):
            in_head = (sub >= h * HEAD_DIM) & (sub < (h + 1) * HEAD_DIM)
            qaugt_ref[h, r, 0:LANES, :] = jnp.where(in_head, q_t, 0.0).astype(BF16)
            first_lane = (pair * HEADS_PER_STEP + h) * DECAY_LANES
            own_decay = (sub >= first_lane) & (sub < first_lane + DECAY_LANES)
            qaugt_ref[h, r, LANES:2 * LANES, :] = jnp.where(own_decay, d_t, 0.0).astype(BF16)
            vt_ref[r, h * V_ROWS:h * V_ROWS + HEAD_DIM, :] = v_t[h * HEAD_DIM:(h + 1) * HEAD_DIM, :].astype(BF16)
            vt_ref[r, h * V_ROWS + HEAD_DIM:(h + 1) * V_ROWS, :] = jnp.ones((V_ROWS - HEAD_DIM, t), BF16)
        return 0

    lax.fori_loop(0, n_tiles, prologue, 0)

    k_idx = lax.broadcasted_iota(jnp.int32, (t, t), 0)
    q_idx = lax.broadcasted_iota(jnp.int32, (t, t), 1)
    bias_ref[...] = jnp.where(k_idx <= q_idx, jnp.inf, NEG_INF)
    acc_ref[...] = jnp.zeros_like(acc_ref)
    m_ref[...] = jnp.full(m_ref.shape, NEG_INF, F32)

    def qk(i, j):
        kaug = jnp.concatenate([k_ref[0, j * t:(j + 1) * t, :], kdec_ref[j * t:(j + 1) * t, :]], axis=1)
        return [jnp.dot(kaug, qaugt_ref[h, i], preferred_element_type=F32) for h in range(HEADS_PER_STEP)]

    def store_scores(slot, i, j, s):
        blk_max = []
        for h in range(HEADS_PER_STEP):
            s_h = jnp.minimum(s[h], bias_ref[...]) if i == j else s[h]
            blk_max.append(jnp.max(s_h, axis=0, keepdims=True))
            s_ref[slot, h] = s_h
        return blk_max

    def softmax_pv(slot, i, j, blk_max):
        for h in range(HEADS_PER_STEP):
            m_old = m_ref[i, h:h + 1, :]
            m_new = jnp.maximum(m_old, blk_max[h])
            p = jnp.exp2(s_ref[slot, h] - m_new)
            alpha = jnp.exp2(m_old - m_new)
            hrows = slice(h * V_ROWS, (h + 1) * V_ROWS)
            pv = jnp.dot(vt_ref[j, hrows, :], p.astype(BF16), preferred_element_type=F32)
            acc_ref[i, hrows, :] = alpha * acc_ref[i, hrows, :] + pv
            m_ref[i, h:h + 1, :] = m_new

    blocks = [(i, j) for i in range(n_tiles) for j in range(i + 1)]
    maxes = [store_scores(n, *blocks[n], qk(*blocks[n])) for n in range(ATT_PIPE)]
    for n, (i, j) in enumerate(blocks):
        slot = n % ATT_PIPE
        ahead = n + ATT_PIPE
        if ahead < len(blocks):
            s_next = qk(*blocks[ahead])
        softmax_pv(slot, i, j, maxes[slot])
        if ahead < len(blocks):
            maxes[slot] = store_scores(slot, *blocks[ahead], s_next)

    def finalize(i, _):
        parts = []
        for h in range(HEADS_PER_STEP):
            row_sum = acc_ref[i, h * V_ROWS + HEAD_DIM:h * V_ROWS + HEAD_DIM + 1, :]
            parts.append(acc_ref[i, h * V_ROWS:h * V_ROWS + HEAD_DIM, :] * (1.0 / row_sum))
        rows = pl.ds(pl.multiple_of(i * t, t), t)
        out_t = jnp.concatenate(parts, axis=0)
        o_ref[0, rows, :] = (out_t.T * gate_ref[0, rows, :].astype(F32)).astype(BF16)
        return 0

    lax.fori_loop(0, n_tiles, finalize, 0)


def _attention(q, k, v, kdec, qdec, gate, batch, seq_len):
    n_pairs, m, _ = q.shape
    t = ATT_TILE
    slab = pl.BlockSpec((1, seq_len, LANES), lambda b, p: (p, b, 0))
    shared = pl.BlockSpec((seq_len, LANES), lambda b, p: (b, 0))
    return pl.pallas_call(
        _attn_kernel,
        grid=(batch, n_pairs),
        in_specs=[slab, slab, slab, shared, shared, slab],
        out_specs=slab,
        out_shape=jax.ShapeDtypeStruct((n_pairs, m, LANES), BF16),
        scratch_shapes=[
            pltpu.VMEM((HEADS_PER_STEP, seq_len // t, 2 * LANES, t), BF16),
            pltpu.VMEM((seq_len // t, HEADS_PER_STEP * V_ROWS, t), BF16),
            pltpu.VMEM((ATT_PIPE, HEADS_PER_STEP, t, t), F32),
            pltpu.VMEM((t, t), F32),
            pltpu.VMEM((seq_len // t, HEADS_PER_STEP * V_ROWS, t), F32),
            pltpu.VMEM((seq_len // t, HEADS_PER_STEP, t), F32),
        ],
        compiler_params=pltpu.CompilerParams(
            dimension_semantics=("arbitrary", "arbitrary"), vmem_limit_bytes=VMEM_LIMIT),
        name="fox_attention",
    )(q, k, v, kdec, qdec, gate)


def _conv_out_kernel(u_ref, gate_ref, w_ref, b_ref, lng_ref, lnb_ref, x_ref, a_ref, wout_ref, o_ref,
                     ext_ref, sh_ref, y_ref):
    ts = u_ref.shape[0]
    n_lt = ext_ref.shape[0]
    s_idx = pl.program_id(1)

    @pl.when(s_idx == 0)
    def _():
        ext_ref[:, 0:CONV_HALO, :] = jnp.zeros((n_lt, CONV_HALO, LANES), F32)

    @pl.when(s_idx != 0)
    def _():
        ext_ref[:, 0:CONV_HALO, :] = ext_ref[:, ts:ts + CONV_HALO, :]

    for lt in range(n_lt):
        ext_ref[lt, CONV_HALO:CONV_HALO + ts, :] = u_ref[:, lt * LANES:(lt + 1) * LANES].astype(F32)

    rc = CONV_CHUNK
    first = CONV_HALO - (CONV_KERNEL - 1)
    sh_rows = sh_ref.shape[1]

    def lane_tile(lt, _):
        for b in range(1, SUBLANES):
            sh_ref[b - 1] = ext_ref[lt, b:b + sh_rows, :]

        def chunk(c, _):
            r0 = pl.multiple_of(c * rc, rc)
            acc = jnp.zeros((rc, LANES), F32) + b_ref[lt]
            for j in range(CONV_KERNEL):
                a, b = divmod(first + j, SUBLANES)
                rows = pl.ds(r0 + a * SUBLANES, rc)
                tap = ext_ref[lt, rows, :] if b == 0 else sh_ref[b - 1, rows, :]
                acc = acc + tap * w_ref[lt, j:j + 1, :]
            y_ref[lt, pl.ds(r0, rc), :] = acc
            return 0

        lax.fori_loop(0, ts // rc, chunk, 0)
        return 0

    lax.fori_loop(0, n_lt, lane_tile, 0)

    y = jnp.concatenate([y_ref[lt] for lt in range(n_lt)], axis=1)
    mu = jnp.mean(y, axis=-1, keepdims=True)
    yc = y - mu
    var = jnp.mean(yc * yc, axis=-1, keepdims=True)
    z = yc * lax.rsqrt(var + EPS) * lng_ref[...] + lnb_ref[...]
    yu = (_silu(z) * gate_ref[...].astype(F32)).astype(BF16)
    a = jnp.concatenate([a_ref[p] for p in range(a_ref.shape[0])], axis=1)
    half = a.shape[1]
    o_ref[...] = (x_ref[...]
                  + jnp.dot(a, wout_ref[0:half, :], preferred_element_type=F32)
                  + jnp.dot(yu, wout_ref[half:, :], preferred_element_type=F32))


def _conv_outproj(u, gate, conv_w, conv_b, ln_g, ln_b, x2d, a, w_out, batch, seq_len):
    m, width = u.shape
    d = x2d.shape[1]
    ts = CONV_TILE
    n_s = seq_len // ts
    n_lt = width // LANES
    w_tiles = conv_w.reshape(CONV_KERNEL, n_lt, LANES).transpose(1, 0, 2)
    b_tiles = conv_b.reshape(n_lt, 1, LANES)
    row = lambda b, s: (b * n_s + s, 0)
    const2 = lambda b, s: (0, 0)
    const3 = lambda b, s: (0, 0, 0)
    return pl.pallas_call(
        _conv_out_kernel,
        grid=(batch, n_s),
        in_specs=[pl.BlockSpec((ts, width), row), pl.BlockSpec((ts, width), row),
                  pl.BlockSpec(w_tiles.shape, const3), pl.BlockSpec(b_tiles.shape, const3),
                  pl.BlockSpec((1, width), const2), pl.BlockSpec((1, width), const2),
                  pl.BlockSpec((ts, d), row),
                  pl.BlockSpec((a.shape[0], ts, LANES), lambda b, s: (0, b * n_s + s, 0)),
                  pl.BlockSpec(w_out.shape, const2)],
        out_specs=pl.BlockSpec((ts, d), row),
        out_shape=jax.ShapeDtypeStruct((m, d), F32),
        scratch_shapes=[pltpu.VMEM((n_lt, CONV_HALO + ts, LANES), F32),
                        pltpu.VMEM((SUBLANES - 1, CONV_HALO + ts - SUBLANES, LANES), F32),
                        pltpu.VMEM((n_lt, ts, LANES), F32)],
        compiler_params=pltpu.CompilerParams(
            dimension_semantics=("arbitrary", "arbitrary"), vmem_limit_bytes=VMEM_LIMIT),
        name="conv_outproj",
    )(u, gate, w_tiles, b_tiles, ln_g, ln_b, x2d, a, w_out)


def kernel(x, norm_g, w_in, b_forget, q_norm_g, k_norm_g, conv_w, conv_b, conv_ln_g, conv_ln_b, w_out):
    batch, seq_len, d_model = x.shape
    depth = w_in.shape[0]
    n_heads = b_forget.shape[1]
    fox_width = n_heads * HEAD_DIM
    conv_width = conv_w.shape[2]
    assert fox_width == d_model and conv_width == d_model
    assert seq_len % ROW_TILE == 0 and seq_len % ATT_TILE == 0 and seq_len % CONV_TILE == 0
    assert n_heads * DECAY_LANES <= LANES and CONV_HALO >= CONV_KERNEL - 1

    o_f = 3 * fox_width
    o_gf = o_f + n_heads
    x2d = x.reshape(batch * seq_len, d_model)
    for l in range(depth):
        w = w_in[l].astype(BF16)
        w_main = jnp.concatenate([w[:, :o_f], w[:, o_gf:]], axis=1)
        w_f = jnp.pad(w[:, o_f:o_gf], ((0, 0), (0, LANES - n_heads)))
        b_f = jnp.pad(b_forget[l], (0, LANES - n_heads)).reshape(1, LANES)
        q, k, v, sgf, u, sgc, kdec, qdec = _inproj(
            x2d, norm_g[l].reshape(1, d_model), w_main, w_f, b_f,
            q_norm_g[l].reshape(1, fox_width), k_norm_g[l].reshape(1, fox_width), n_heads, seq_len)
        a = _attention(q, k, v, kdec, qdec, sgf, batch, seq_len)
        x2d = _conv_outproj(u, sgc, conv_w[l], conv_b[l],
                            conv_ln_g[l].reshape(1, conv_width), conv_ln_b[l].reshape(1, conv_width),
                            x2d, a, w_out[l].astype(BF16), batch, seq_len)
    return x2d.reshape(batch, seq_len, d_model)
```

```python
import functools
import math

import jax
import jax.numpy as jnp
import numpy as np
from jax import lax
from jax.experimental import pallas as pl
from jax.experimental.pallas import tpu as pltpu

F32 = jnp.float32
BF16 = jnp.bfloat16

HEAD_DIM = 64
CONV_KERNEL = 31
EPS = 1e-6
NEG_INF = -1e30
LOG2E = math.log2(math.e)

LANES = 128
SUBLANES = 8
BF16_ROWS = 16
HEADS_PER_STEP = LANES // HEAD_DIM
DECAY_LANES = 8
VMEM_LIMIT = 56 * 1024 * 1024

ROW_TILE = 512
ATT_TILE = 256
ATT_PIPE = 2
V_ROWS = HEAD_DIM + BF16_ROWS
CONV_TILE = 256
CONV_HALO = 32
CONV_CHUNK = 128


def _split3(x):
    hi = x.astype(BF16)
    r1 = x - hi.astype(F32)
    mid = r1.astype(BF16)
    lo = (r1 - mid.astype(F32)).astype(BF16)
    return hi, mid, lo


def _sigmoid(x):
    return 0.5 * jnp.tanh(0.5 * x) + 0.5


def _silu(x):
    h = 0.5 * x
    return h + h * jnp.tanh(h)


def _head_norm(xf, gain):
    lo_half = lax.broadcasted_iota(jnp.int32, (1, LANES), 1) < HEAD_DIM
    sq = xf * xf
    s_lo = jnp.sum(jnp.where(lo_half, sq, 0.0), axis=-1, keepdims=True)
    s_hi = jnp.sum(jnp.where(lo_half, 0.0, sq), axis=-1, keepdims=True)
    mean_sq = jnp.where(lo_half, s_lo, s_hi) * (1.0 / HEAD_DIM)
    return xf * lax.rsqrt(mean_sq + EPS) * gain


def _inproj_kernel(x_ref, g_ref, w_ref, wf_ref, bf_ref, tri_ref, gq_ref, gk_ref, pk_ref, pq_ref,
                   q_ref, k_ref, v_ref, sgf_ref, u_ref, sgc_ref, kdec_ref, qdec_ref,
                   carry_ref, *, tiles_per_seq, width):
    i = pl.program_id(0)
    n_lt = width // LANES
    x = x_ref[...]
    ms = jnp.mean(x * x, axis=-1, keepdims=True)
    h = (x * lax.rsqrt(ms + EPS) * g_ref[...]).astype(BF16)

    def mm(sec):
        return jnp.dot(h, w_ref[:, sec * width:(sec + 1) * width],
                       preferred_element_type=F32)

    def lane_tile(a, p):
        return a[:, p * LANES:(p + 1) * LANES]

    q = mm(0)
    for p in range(n_lt):
        qn = _head_norm(lane_tile(q, p), gq_ref[:, p * LANES:(p + 1) * LANES])
        q_ref[p] = (qn * (LOG2E * HEAD_DIM ** -0.5)).astype(BF16)
    k = mm(1)
    for p in range(n_lt):
        k_ref[p] = _head_norm(lane_tile(k, p), gk_ref[:, p * LANES:(p + 1) * LANES]).astype(BF16)
    v = mm(2)
    gf = mm(3)
    sgf = _silu(gf)
    for p in range(n_lt):
        v_ref[p] = lane_tile(v, p).astype(BF16)
        sgf_ref[p] = lane_tile(sgf, p).astype(BF16)
    glu_a = mm(4)
    glu_b = mm(5)
    u_ref[...] = (glu_a * _sigmoid(glu_b)).astype(BF16)
    gc = mm(6)
    sgc_ref[...] = _silu(gc).astype(BF16)

    zf = jnp.dot(h, wf_ref[...], preferred_element_type=F32) + bf_ref[...]
    log_f = jnp.minimum(zf, 0.0) - jnp.log1p(jnp.exp(-jnp.abs(zf)))
    pieces = jnp.concatenate(_split3(log_f), axis=1)
    cs = jnp.dot(tri_ref[...], pieces, preferred_element_type=F32)
    c_tile = cs[:, :LANES] + cs[:, LANES:2 * LANES] + cs[:, 2 * LANES:]

    @pl.when(i % tiles_per_seq == 0)
    def _():
        carry_ref[...] = jnp.zeros_like(carry_ref)

    c_full = c_tile + carry_ref[...]
    carry_ref[...] = c_full[c_full.shape[0] - 1:, :]

    c_pieces = jnp.concatenate(_split3(c_full * LOG2E), axis=1)
    slot = lax.broadcasted_iota(jnp.int32, (1, LANES), 1) & (DECAY_LANES - 1)
    kdec = jnp.dot(c_pieces, pk_ref[...], preferred_element_type=F32) + jnp.where(slot < 3, 1.0, 0.0)
    qdec = jnp.dot(c_pieces, pq_ref[...], preferred_element_type=F32) + jnp.where((slot >= 3) & (slot < 6), 1.0, 0.0)
    kdec_ref[...] = kdec.astype(BF16)
    qdec_ref[...] = qdec.astype(BF16)


def _decay_placement(n_heads):
    pk = np.zeros((3 * LANES, LANES), np.float32)
    pq = np.zeros((3 * LANES, LANES), np.float32)
    for head in range(n_heads):
        for piece in range(3):
            pq[piece * LANES + head, DECAY_LANES * head + piece] = 1.0
            pk[piece * LANES + head, DECAY_LANES * head + 3 + piece] = -1.0
    return jnp.asarray(pk, BF16), jnp.asarray(pq, BF16)


def _inproj(x2d, norm_g, w_main, w_f, b_f, gq, gk, n_heads, seq_len):
    m, d = x2d.shape
    tm = ROW_TILE
    width = d
    n_lt = width // LANES
    tri = (lax.broadcasted_iota(jnp.int32, (tm, tm), 0)
           >= lax.broadcasted_iota(jnp.int32, (tm, tm), 1)).astype(BF16)
    pk, pq = _decay_placement(n_heads)
    row = lambda i: (i, 0)
    const = lambda i: (0, 0)
    tiled = pl.BlockSpec((n_lt, tm, LANES), lambda i: (0, i, 0))
    out_tiled = jax.ShapeDtypeStruct((n_lt, m, LANES), BF16)
    out_bf = jax.ShapeDtypeStruct((m, width), BF16)
    out_dec = jax.ShapeDtypeStruct((m, LANES), BF16)
    return pl.pallas_call(
        functools.partial(_inproj_kernel, tiles_per_seq=seq_len // tm, width=width),
        grid=(m // tm,),
        in_specs=[
            pl.BlockSpec((tm, d), row),
            pl.BlockSpec((1, d), const),
            pl.BlockSpec(w_main.shape, const, pipeline_mode=pl.Buffered(1)),
            pl.BlockSpec(w_f.shape, const),
            pl.BlockSpec((1, LANES), const),
            pl.BlockSpec((tm, tm), const),
            pl.BlockSpec((1, width), const),
            pl.BlockSpec((1, width), const),
            pl.BlockSpec(pk.shape, const),
            pl.BlockSpec(pq.shape, const),
        ],
        out_specs=[tiled] * 4 + [pl.BlockSpec((tm, width), row)] * 2 + [pl.BlockSpec((tm, LANES), row)] * 2,
        out_shape=[out_tiled] * 4 + [out_bf] * 2 + [out_dec] * 2,
        scratch_shapes=[pltpu.VMEM((1, LANES), F32)],
        compiler_params=pltpu.CompilerParams(
            dimension_semantics=("arbitrary",), vmem_limit_bytes=VMEM_LIMIT),
        name="inproj",
    )(x2d, norm_g, w_main, w_f, b_f, tri, gq, gk, pk, pq)


def _attn_kernel(q_ref, k_ref, v_ref, kdec_ref, qdec_ref, gate_ref, o_ref,
                 qaugt_ref, vt_ref, s_ref, bias_ref, acc_ref, m_ref):
    t = ATT_TILE
    seq = q_ref.shape[1]
    n_tiles = seq // t
    pair = pl.program_id(1)
    sub = lax.broadcasted_iota(jnp.int32, (LANES, 1), 0)

    def prepare(r):
        rows = slice(r * t, (r + 1) * t)
        q_t = q_ref[0, rows, :].astype(F32).T
        d_t = qdec_ref[rows, :].astype(F32).T
        v_t = v_ref[0, rows, :].astype(F32).T
        for h in range(HEADS_PER_STEP):
            in_head = (sub >= h * HEAD_DIM) & (sub < (h + 1) * HEAD_DIM)
            qaugt_ref[h, r, 0:LANES, :] = jnp.where(in_head, q_t, 0.0).astype(BF16)
            first_lane = (pair * HEADS_PER_STEP + h) * DECAY_LANES
            own_decay = (sub >= first_lane) & (sub < first_lane + DECAY_LANES)
            qaugt_ref[h, r, LANES:2 * LANES, :] = jnp.where(own_decay, d_t, 0.0).astype(BF16)
            vt_ref[r, h * V_ROWS:h * V_ROWS + HEAD_DIM, :] = v_t[h * HEAD_DIM:(h + 1) * HEAD_DIM, :].astype(BF16)
            vt_ref[r, h * V_ROWS + HEAD_DIM:(h + 1) * V_ROWS, :] = jnp.ones((V_ROWS - HEAD_DIM, t), BF16)

    k_idx = lax.broadcasted_iota(jnp.int32, (t, t), 0)
    q_idx = lax.broadcasted_iota(jnp.int32, (t, t), 1)
    bias_ref[...] = jnp.where(k_idx <= q_idx, jnp.inf, NEG_INF)
    acc_ref[...] = jnp.zeros_like(acc_ref)
    m_ref[...] = jnp.full(m_ref.shape, NEG_INF, F32)

    def qk(i, j):
        kaug = jnp.concatenate([k_ref[0, j * t:(j + 1) * t, :], kdec_ref[j * t:(j + 1) * t, :]], axis=1)
        return [jnp.dot(kaug, qaugt_ref[h, i], preferred_element_type=F32) for h in range(HEADS_PER_STEP)]

    def store_scores(slot, i, j, s):
        blk_max = []
        for h in range(HEADS_PER_STEP):
            s_h = jnp.minimum(s[h], bias_ref[...]) if i == j else s[h]
            blk_max.append(jnp.max(s_h, axis=0, keepdims=True))
            s_ref[slot, h] = s_h
        return blk_max

    def softmax_pv(slot, i, j, blk_max):
        for h in range(HEADS_PER_STEP):
            m_old = m_ref[i, h:h + 1, :]
            m_new = jnp.maximum(m_old, blk_max[h])
            p = jnp.exp2(s_ref[slot, h] - m_new)
            alpha = jnp.exp2(m_old - m_new)
            hrows = slice(h * V_ROWS, (h + 1) * V_ROWS)
            pv = jnp.dot(vt_ref[j, hrows, :], p.astype(BF16), preferred_element_type=F32)
            acc_ref[i, hrows, :] = alpha * acc_ref[i, hrows, :] + pv
            m_ref[i, h:h + 1, :] = m_new

    def finalize(i):
        parts = []
        for h in range(HEADS_PER_STEP):
            row_sum = acc_ref[i, h * V_ROWS + HEAD_DIM:h * V_ROWS + HEAD_DIM + 1, :]
            parts.append(acc_ref[i, h * V_ROWS:h * V_ROWS + HEAD_DIM, :] * (1.0 / row_sum))
        rows = slice(i * t, (i + 1) * t)
        out_t = jnp.concatenate(parts, axis=0)
        o_ref[0, rows, :] = (out_t.T * gate_ref[0, rows, :].astype(F32)).astype(BF16)

    blocks = [(i, j) for i in range(n_tiles) for j in range(i + 1)]
    prepared = 0

    def issue(n):
        nonlocal prepared
        while prepared <= min(blocks[n][0] + 1, n_tiles - 1):
            prepare(prepared)
            prepared += 1
        return qk(*blocks[n])

    maxes = [store_scores(n, *blocks[n], issue(n)) for n in range(ATT_PIPE)]
    for n, (i, j) in enumerate(blocks):
        slot = n % ATT_PIPE
        ahead = n + ATT_PIPE
        if ahead < len(blocks):
            s_next = issue(ahead)
        softmax_pv(slot, i, j, maxes[slot])
        if ahead < len(blocks):
            maxes[slot] = store_scores(slot, *blocks[ahead], s_next)
        if i == j:
            finalize(i)


def _attention(q, k, v, kdec, qdec, gate, batch, seq_len):
    n_pairs, m, _ = q.shape
    t = ATT_TILE
    slab = pl.BlockSpec((1, seq_len, LANES), lambda b, p: (p, b, 0))
    shared = pl.BlockSpec((seq_len, LANES), lambda b, p: (b, 0))
    return pl.pallas_call(
        _attn_kernel,
        grid=(batch, n_pairs),
        in_specs=[slab, slab, slab, shared, shared, slab],
        out_specs=slab,
        out_shape=jax.ShapeDtypeStruct((n_pairs, m, LANES), BF16),
        scratch_shapes=[
            pltpu.VMEM((HEADS_PER_STEP, seq_len // t, 2 * LANES, t), BF16),
            pltpu.VMEM((seq_len // t, HEADS_PER_STEP * V_ROWS, t), BF16),
            pltpu.VMEM((ATT_PIPE, HEADS_PER_STEP, t, t), F32),
            pltpu.VMEM((t, t), F32),
            pltpu.VMEM((seq_len // t, HEADS_PER_STEP * V_ROWS, t), F32),
            pltpu.VMEM((seq_len // t, HEADS_PER_STEP, t), F32),
        ],
        compiler_params=pltpu.CompilerParams(
            dimension_semantics=("arbitrary", "arbitrary"), vmem_limit_bytes=VMEM_LIMIT),
        name="fox_attention",
    )(q, k, v, kdec, qdec, gate)


def _conv_out_kernel(u_ref, gate_ref, w_ref, b_ref, lng_ref, lnb_ref, x_ref, a_ref, wout_ref, o_ref,
                     ext_ref, sh_ref, y_ref):
    ts = u_ref.shape[0]
    n_lt = ext_ref.shape[0]
    s_idx = pl.program_id(1)

    @pl.when(s_idx == 0)
    def _():
        ext_ref[:, 0:CONV_HALO, :] = jnp.zeros((n_lt, CONV_HALO, LANES), F32)

    @pl.when(s_idx != 0)
    def _():
        ext_ref[:, 0:CONV_HALO, :] = ext_ref[:, ts:ts + CONV_HALO, :]

    for lt in range(n_lt):
        ext_ref[lt, CONV_HALO:CONV_HALO + ts, :] = u_ref[:, lt * LANES:(lt + 1) * LANES].astype(F32)

    rc = CONV_CHUNK
    first = CONV_HALO - (CONV_KERNEL - 1)
    sh_rows = sh_ref.shape[1]

    def lane_tile(lt, _):
        for b in range(1, SUBLANES):
            sh_ref[b - 1] = ext_ref[lt, b:b + sh_rows, :]

        def chunk(c, _):
            r0 = pl.multiple_of(c * rc, rc)
            acc = jnp.zeros((rc, LANES), F32) + b_ref[lt]
            for j in range(CONV_KERNEL):
                a, b = divmod(first + j, SUBLANES)
                rows = pl.ds(r0 + a * SUBLANES, rc)
                tap = ext_ref[lt, rows, :] if b == 0 else sh_ref[b - 1, rows, :]
                acc = acc + tap * w_ref[lt, j:j + 1, :]
            y_ref[lt, pl.ds(r0, rc), :] = acc
            return 0

        lax.fori_loop(0, ts // rc, chunk, 0)
        return 0

    lax.fori_loop(0, n_lt, lane_tile, 0)

    y = jnp.concatenate([y_ref[lt] for lt in range(n_lt)], axis=1)
    mu = jnp.mean(y, axis=-1, keepdims=True)
    yc = y - mu
    var = jnp.mean(yc * yc, axis=-1, keepdims=True)
    z = yc * lax.rsqrt(var + EPS) * lng_ref[...] + lnb_ref[...]
    yu = (_silu(z) * gate_ref[...].astype(F32)).astype(BF16)
    a = jnp.concatenate([a_ref[p] for p in range(a_ref.shape[0])], axis=1)
    half = a.shape[1]
    o_ref[...] = (x_ref[...]
                  + jnp.dot(a, wout_ref[0:half, :], preferred_element_type=F32)
                  + jnp.dot(yu, wout_ref[half:, :], preferred_element_type=F32))


def _conv_outproj(u, gate, conv_w, conv_b, ln_g, ln_b, x2d, a, w_out, batch, seq_len):
    m, width = u.shape
    d = x2d.shape[1]
    ts = CONV_TILE
    n_s = seq_len // ts
    n_lt = width // LANES
    w_tiles = conv_w.reshape(CONV_KERNEL, n_lt, LANES).transpose(1, 0, 2)
    b_tiles = conv_b.reshape(n_lt, 1, LANES)
    row = lambda b, s: (b * n_s + s, 0)
    const2 = lambda b, s: (0, 0)
    const3 = lambda b, s: (0, 0, 0)
    return pl.pallas_call(
        _conv_out_kernel,
        grid=(batch, n_s),
        in_specs=[pl.BlockSpec((ts, width), row), pl.BlockSpec((ts, width), row),
                  pl.BlockSpec(w_tiles.shape, const3), pl.BlockSpec(b_tiles.shape, const3),
                  pl.BlockSpec((1, width), const2), pl.BlockSpec((1, width), const2),
                  pl.BlockSpec((ts, d), row),
                  pl.BlockSpec((a.shape[0], ts, LANES), lambda b, s: (0, b * n_s + s, 0)),
                  pl.BlockSpec(w_out.shape, const2)],
        out_specs=pl.BlockSpec((ts, d), row),
        out_shape=jax.ShapeDtypeStruct((m, d), F32),
        scratch_shapes=[pltpu.VMEM((n_lt, CONV_HALO + ts, LANES), F32),
                        pltpu.VMEM((SUBLANES - 1, CONV_HALO + ts - SUBLANES, LANES), F32),
                        pltpu.VMEM((n_lt, ts, LANES), F32)],
        compiler_params=pltpu.CompilerParams(
            dimension_semantics=("arbitrary", "arbitrary"), vmem_limit_bytes=VMEM_LIMIT),
        name="conv_outproj",
    )(u, gate, w_tiles, b_tiles, ln_g, ln_b, x2d, a, w_out)


def kernel(x, norm_g, w_in, b_forget, q_norm_g, k_norm_g, conv_w, conv_b, conv_ln_g, conv_ln_b, w_out):
    batch, seq_len, d_model = x.shape
    depth = w_in.shape[0]
    n_heads = b_forget.shape[1]
    fox_width = n_heads * HEAD_DIM
    conv_width = conv_w.shape[2]
    assert fox_width == d_model and conv_width == d_model
    assert seq_len % ROW_TILE == 0 and seq_len % ATT_TILE == 0 and seq_len % CONV_TILE == 0
    assert n_heads * DECAY_LANES <= LANES and CONV_HALO >= CONV_KERNEL - 1

    o_f = 3 * fox_width
    o_gf = o_f + n_heads
    x2d = x.reshape(batch * seq_len, d_model)
    for l in range(depth):
        w = w_in[l].astype(BF16)
        w_main = jnp.concatenate([w[:, :o_f], w[:, o_gf:]], axis=1)
        w_f = jnp.pad(w[:, o_f:o_gf], ((0, 0), (0, LANES - n_heads)))
        b_f = jnp.pad(b_forget[l], (0, LANES - n_heads)).reshape(1, LANES)
        q, k, v, sgf, u, sgc, kdec, qdec = _inproj(
            x2d, norm_g[l].reshape(1, d_model), w_main, w_f, b_f,
            q_norm_g[l].reshape(1, fox_width), k_norm_g[l].reshape(1, fox_width), n_heads, seq_len)
        a = _attention(q, k, v, kdec, qdec, sgf, batch, seq_len)
        x2d = _conv_outproj(u, sgc, conv_w[l], conv_b[l],
                            conv_ln_g[l].reshape(1, conv_width), conv_ln_b[l].reshape(1, conv_width),
                            x2d, a, w_out[l].astype(BF16), batch, seq_len)
    return x2d.reshape(batch, seq_len, d_model)
```

```python
import functools
import math

import jax
import jax.numpy as jnp
import numpy as np
from jax import lax
from jax.experimental import pallas as pl
from jax.experimental.pallas import tpu as pltpu

F32 = jnp.float32
BF16 = jnp.bfloat16

HEAD_DIM = 64
CONV_KERNEL = 31
EPS = 1e-6
NEG_INF = -1e30
LOG2E = math.log2(math.e)

LANES = 128
SUBLANES = 8
BF16_ROWS = 16
HEADS_PER_STEP = LANES // HEAD_DIM
DECAY_LANES = 8
VMEM_LIMIT = 56 * 1024 * 1024

ROW_TILE = 512
ATT_TILE = 256
ATT_PIPE = 2
V_ROWS = HEAD_DIM + BF16_ROWS
CONV_TILE = 256
CONV_HALO = 32
CONV_CHUNK = 128


def _split3(x):
    hi = x.astype(BF16)
    r1 = x - hi.astype(F32)
    mid = r1.astype(BF16)
    lo = (r1 - mid.astype(F32)).astype(BF16)
    return hi, mid, lo


def _sigmoid(x):
    return 0.5 * jnp.tanh(0.5 * x) + 0.5


def _silu(x):
    h = 0.5 * x
    return h + h * jnp.tanh(h)


def _head_norm(xf, gain):
    lo_half = lax.broadcasted_iota(jnp.int32, (1, LANES), 1) < HEAD_DIM
    sq = xf * xf
    s_lo = jnp.sum(jnp.where(lo_half, sq, 0.0), axis=-1, keepdims=True)
    s_hi = jnp.sum(jnp.where(lo_half, 0.0, sq), axis=-1, keepdims=True)
    mean_sq = jnp.where(lo_half, s_lo, s_hi) * (1.0 / HEAD_DIM)
    return xf * lax.rsqrt(mean_sq + EPS) * gain


def _inproj_kernel(x_ref, g_ref, w_ref, wf_ref, bf_ref, tri_ref, gq_ref, gk_ref, place_ref,
                   q_ref, k_ref, v_ref, sgf_ref, u_ref, sgc_ref, kdec_ref, qdec_ref,
                   carry_ref, *, tiles_per_seq, width):
    i = pl.program_id(0)
    n_lt = width // LANES
    @pl.when(i % tiles_per_seq == 0)
    def _():
        carry_ref[...] = jnp.zeros_like(carry_ref)

    x = x_ref[...]
    inv_rms = lax.rsqrt(jnp.mean(x * x, axis=-1, keepdims=True) + EPS)
    h = (x * g_ref[...]).astype(BF16)

    def mm(sec):
        return inv_rms * jnp.dot(h, w_ref[:, sec * width:(sec + 1) * width], preferred_element_type=F32)

    def lane_tile(a, p):
        return a[:, p * LANES:(p + 1) * LANES]

    zf = inv_rms * jnp.dot(h, wf_ref[...], preferred_element_type=F32) + bf_ref[...]

    q = mm(0)
    for p in range(n_lt):
        qn = _head_norm(lane_tile(q, p), gq_ref[:, p * LANES:(p + 1) * LANES])
        q_ref[p] = (qn * (LOG2E * HEAD_DIM ** -0.5)).astype(BF16)

    log_f = jnp.minimum(zf, 0.0) - jnp.log1p(jnp.exp(-jnp.abs(zf)))
    pieces = jnp.concatenate(_split3(log_f), axis=1)
    cs = jnp.dot(tri_ref[...], pieces, preferred_element_type=F32)
    c_full = cs[:, :LANES] + cs[:, LANES:2 * LANES] + cs[:, 2 * LANES:] + carry_ref[...]
    carry_ref[...] = c_full[c_full.shape[0] - 1:, :]
    c_pieces = jnp.concatenate(_split3(c_full * LOG2E), axis=1)

    k = mm(1)
    for p in range(n_lt):
        k_ref[p] = _head_norm(lane_tile(k, p), gk_ref[:, p * LANES:(p + 1) * LANES]).astype(BF16)

    slot = lax.broadcasted_iota(jnp.int32, (1, LANES), 1) & (DECAY_LANES - 1)
    dec = jnp.dot(c_pieces, place_ref[...], preferred_element_type=F32)
    kdec_ref[...] = (dec[:, :LANES] + jnp.where(slot < 3, 1.0, 0.0)).astype(BF16)
    qdec_ref[...] = (dec[:, LANES:] + jnp.where((slot >= 3) & (slot < 6), 1.0, 0.0)).astype(BF16)

    v = mm(2)
    gf = mm(3)
    sgf = _silu(gf)
    for p in range(n_lt):
        v_ref[p] = lane_tile(v, p).astype(BF16)
        sgf_ref[p] = lane_tile(sgf, p).astype(BF16)
    glu_a = mm(4)
    glu_b = mm(5)
    u_ref[...] = (glu_a * _sigmoid(glu_b)).astype(BF16)
    gc = mm(6)
    sgc_ref[...] = _silu(gc).astype(BF16)


def _decay_placement(n_heads):
    pk = np.zeros((3 * LANES, LANES), np.float32)
    pq = np.zeros((3 * LANES, LANES), np.float32)
    for head in range(n_heads):
        for piece in range(3):
            pq[piece * LANES + head, DECAY_LANES * head + piece] = 1.0
            pk[piece * LANES + head, DECAY_LANES * head + 3 + piece] = -1.0
    return jnp.asarray(np.concatenate([pk, pq], axis=1), BF16)


def _inproj(x2d, norm_g, w_main, w_f, b_f, gq, gk, n_heads, seq_len):
    m, d = x2d.shape
    tm = ROW_TILE
    width = d
    n_lt = width // LANES
    tri = (lax.broadcasted_iota(jnp.int32, (tm, tm), 0)
           >= lax.broadcasted_iota(jnp.int32, (tm, tm), 1)).astype(BF16)
    place = _decay_placement(n_heads)
    row = lambda i: (i, 0)
    const = lambda i: (0, 0)
    tiled = pl.BlockSpec((n_lt, tm, LANES), lambda i: (0, i, 0))
    out_tiled = jax.ShapeDtypeStruct((n_lt, m, LANES), BF16)
    out_bf = jax.ShapeDtypeStruct((m, width), BF16)
    out_dec = jax.ShapeDtypeStruct((m, LANES), BF16)
    return pl.pallas_call(
        functools.partial(_inproj_kernel, tiles_per_seq=seq_len // tm, width=width),
        grid=(m // tm,),
        in_specs=[
            pl.BlockSpec((tm, d), row),
            pl.BlockSpec((1, d), const),
            pl.BlockSpec(w_main.shape, const, pipeline_mode=pl.Buffered(1)),
            pl.BlockSpec(w_f.shape, const),
            pl.BlockSpec((1, LANES), const),
            pl.BlockSpec((tm, tm), const),
            pl.BlockSpec((1, width), const),
            pl.BlockSpec((1, width), const),
            pl.BlockSpec(place.shape, const),
        ],
        out_specs=[tiled] * 4 + [pl.BlockSpec((tm, width), row)] * 2 + [pl.BlockSpec((tm, LANES), row)] * 2,
        out_shape=[out_tiled] * 4 + [out_bf] * 2 + [out_dec] * 2,
        scratch_shapes=[pltpu.VMEM((1, LANES), F32)],
        compiler_params=pltpu.CompilerParams(
            dimension_semantics=("arbitrary",), vmem_limit_bytes=VMEM_LIMIT),
        name="inproj",
    )(x2d, norm_g, w_main, w_f, b_f, tri, gq, gk, place)


def _attn_kernel(q_ref, k_ref, v_ref, kdec_ref, qdec_ref, gate_ref, o_ref,
                 qaugt_ref, vt_ref, s_ref, bias_ref, acc_ref, m_ref):
    t = ATT_TILE
    seq = q_ref.shape[1]
    n_tiles = seq // t
    pair = pl.program_id(1)
    sub = lax.broadcasted_iota(jnp.int32, (LANES, 1), 0)

    def prepare(r):
        rows = slice(r * t, (r + 1) * t)
        q_t = q_ref[0, rows, :].astype(F32).T
        d_t = qdec_ref[rows, :].astype(F32).T
        v_t = v_ref[0, rows, :].astype(F32).T
        for h in range(HEADS_PER_STEP):
            in_head = (sub >= h * HEAD_DIM) & (sub < (h + 1) * HEAD_DIM)
            qaugt_ref[h, r, 0:LANES, :] = jnp.where(in_head, q_t, 0.0).astype(BF16)
            first_lane = (pair * HEADS_PER_STEP + h) * DECAY_LANES
            own_decay = (sub >= first_lane) & (sub < first_lane + DECAY_LANES)
            qaugt_ref[h, r, LANES:2 * LANES, :] = jnp.where(own_decay, d_t, 0.0).astype(BF16)
            vt_ref[r, h * V_ROWS:h * V_ROWS + HEAD_DIM, :] = v_t[h * HEAD_DIM:(h + 1) * HEAD_DIM, :].astype(BF16)
            vt_ref[r, h * V_ROWS + HEAD_DIM:(h + 1) * V_ROWS, :] = jnp.ones((V_ROWS - HEAD_DIM, t), BF16)

    k_idx = lax.broadcasted_iota(jnp.int32, (t, t), 0)
    q_idx = lax.broadcasted_iota(jnp.int32, (t, t), 1)
    bias_ref[...] = jnp.where(k_idx <= q_idx, jnp.inf, NEG_INF)
    acc_ref[...] = jnp.zeros_like(acc_ref)
    m_ref[...] = jnp.full(m_ref.shape, NEG_INF, F32)

    def qk(i, j):
        kaug = jnp.concatenate([k_ref[0, j * t:(j + 1) * t, :], kdec_ref[j * t:(j + 1) * t, :]], axis=1)
        return [jnp.dot(kaug, qaugt_ref[h, i], preferred_element_type=F32) for h in range(HEADS_PER_STEP)]

    def store_scores(slot, i, j, s):
        blk_max = []
        for h in range(HEADS_PER_STEP):
            s_h = jnp.minimum(s[h], bias_ref[...]) if i == j else s[h]
            blk_max.append(jnp.max(s_h, axis=0, keepdims=True))
            s_ref[slot, h] = s_h
        return blk_max

    def softmax_pv(slot, i, j, blk_max):
        for h in range(HEADS_PER_STEP):
            m_old = m_ref[i, h:h + 1, :]
            m_new = jnp.maximum(m_old, blk_max[h])
            p = jnp.exp2(s_ref[slot, h] - m_new)
            alpha = jnp.exp2(m_old - m_new)
            hrows = slice(h * V_ROWS, (h + 1) * V_ROWS)
            pv = jnp.dot(vt_ref[j, hrows, :], p.astype(BF16), preferred_element_type=F32)
            acc_ref[i, hrows, :] = alpha * acc_ref[i, hrows, :] + pv
            m_ref[i, h:h + 1, :] = m_new

    def finalize(i):
        parts = []
        for h in range(HEADS_PER_STEP):
            row_sum = acc_ref[i, h * V_ROWS + HEAD_DIM:h * V_ROWS + HEAD_DIM + 1, :]
            parts.append(acc_ref[i, h * V_ROWS:h * V_ROWS + HEAD_DIM, :] * (1.0 / row_sum))
        rows = slice(i * t, (i + 1) * t)
        out_t = jnp.concatenate(parts, axis=0)
        o_ref[0, rows, :] = (out_t.T * gate_ref[0, rows, :].astype(F32)).astype(BF16)

    blocks = [(i, j) for i in range(n_tiles) for j in range(i + 1)]
    prepared = 0

    def issue(n):
        nonlocal prepared
        while prepared <= min(blocks[n][0] + 1, n_tiles - 1):
            prepare(prepared)
            prepared += 1
        return qk(*blocks[n])

    maxes = [store_scores(n, *blocks[n], issue(n)) for n in range(ATT_PIPE)]
    for n, (i, j) in enumerate(blocks):
        slot = n % ATT_PIPE
        ahead = n + ATT_PIPE
        if ahead < len(blocks):
            s_next = issue(ahead)
        softmax_pv(slot, i, j, maxes[slot])
        if ahead < len(blocks):
            maxes[slot] = store_scores(slot, *blocks[ahead], s_next)
        if i == j:
            finalize(i)


def _attention(q, k, v, kdec, qdec, gate, batch, seq_len):
    n_pairs, m, _ = q.shape
    t = ATT_TILE
    slab = pl.BlockSpec((1, seq_len, LANES), lambda b, p: (p, b, 0))
    shared = pl.BlockSpec((seq_len, LANES), lambda b, p: (b, 0))
    return pl.pallas_call(
        _attn_kernel,
        grid=(batch, n_pairs),
        in_specs=[slab, slab, slab, shared, shared, slab],
        out_specs=slab,
        out_shape=jax.ShapeDtypeStruct((n_pairs, m, LANES), BF16),
        scratch_shapes=[
            pltpu.VMEM((HEADS_PER_STEP, seq_len // t, 2 * LANES, t), BF16),
            pltpu.VMEM((seq_len // t, HEADS_PER_STEP * V_ROWS, t), BF16),
            pltpu.VMEM((ATT_PIPE, HEADS_PER_STEP, t, t), F32),
            pltpu.VMEM((t, t), F32),
            pltpu.VMEM((seq_len // t, HEADS_PER_STEP * V_ROWS, t), F32),
            pltpu.VMEM((seq_len // t, HEADS_PER_STEP, t), F32),
        ],
        compiler_params=pltpu.CompilerParams(
            dimension_semantics=("arbitrary", "arbitrary"), vmem_limit_bytes=VMEM_LIMIT),
        name="fox_attention",
    )(q, k, v, kdec, qdec, gate)


def _conv_out_kernel(u_ref, gate_ref, w_ref, b_ref, lng_ref, lnb_ref, x_ref, a_ref, wout_ref, o_ref,
                     ext_ref, sh_ref, y_ref):
    ts = u_ref.shape[0]
    n_lt = ext_ref.shape[0]
    s_idx = pl.program_id(1)

    @pl.when(s_idx == 0)
    def _():
        ext_ref[:, 0:CONV_HALO, :] = jnp.zeros((n_lt, CONV_HALO, LANES), F32)

    @pl.when(s_idx != 0)
    def _():
        ext_ref[:, 0:CONV_HALO, :] = ext_ref[:, ts:ts + CONV_HALO, :]

    for lt in range(n_lt):
        ext_ref[lt, CONV_HALO:CONV_HALO + ts, :] = u_ref[:, lt * LANES:(lt + 1) * LANES].astype(F32)

    rc = CONV_CHUNK
    first = CONV_HALO - (CONV_KERNEL - 1)
    sh_rows = sh_ref.shape[1]

    def lane_tile(lt, _):
        for b in range(1, SUBLANES):
            sh_ref[b - 1] = ext_ref[lt, b:b + sh_rows, :]

        def chunk(c, _):
            r0 = pl.multiple_of(c * rc, rc)
            acc = jnp.zeros((rc, LANES), F32) + b_ref[lt]
            for j in range(CONV_KERNEL):
                a, b = divmod(first + j, SUBLANES)
                rows = pl.ds(r0 + a * SUBLANES, rc)
                tap = ext_ref[lt, rows, :] if b == 0 else sh_ref[b - 1, rows, :]
                acc = acc + tap * w_ref[lt, j:j + 1, :]
            y_ref[lt, pl.ds(r0, rc), :] = acc
            return 0

        lax.fori_loop(0, ts // rc, chunk, 0)
        return 0

    lax.fori_loop(0, n_lt, lane_tile, 0)

    y = jnp.concatenate([y_ref[lt] for lt in range(n_lt)], axis=1)
    mu = jnp.mean(y, axis=-1, keepdims=True)
    yc = y - mu
    var = jnp.mean(yc * yc, axis=-1, keepdims=True)
    z = yc * lax.rsqrt(var + EPS) * lng_ref[...] + lnb_ref[...]
    yu = (_silu(z) * gate_ref[...].astype(F32)).astype(BF16)
    a = jnp.concatenate([a_ref[p] for p in range(a_ref.shape[0])], axis=1)
    half = a.shape[1]
    o_ref[...] = (x_ref[...]
                  + jnp.dot(a, wout_ref[0:half, :], preferred_element_type=F32)
                  + jnp.dot(yu, wout_ref[half:, :], preferred_element_type=F32))


def _conv_outproj(u, gate, conv_w, conv_b, ln_g, ln_b, x2d, a, w_out, batch, seq_len):
    m, width = u.shape
    d = x2d.shape[1]
    ts = CONV_TILE
    n_s = seq_len // ts
    n_lt = width // LANES
    w_tiles = conv_w.reshape(CONV_KERNEL, n_lt, LANES).transpose(1, 0, 2)
    b_tiles = conv_b.reshape(n_lt, 1, LANES)
    row = lambda b, s: (b * n_s + s, 0)
    const2 = lambda b, s: (0, 0)
    const3 = lambda b, s: (0, 0, 0)
    return pl.pallas_call(
        _conv_out_kernel,
        grid=(batch, n_s),
        in_specs=[pl.BlockSpec((ts, width), row), pl.BlockSpec((ts, width), row),
                  pl.BlockSpec(w_tiles.shape, const3), pl.BlockSpec(b_tiles.shape, const3),
                  pl.BlockSpec((1, width), const2), pl.BlockSpec((1, width), const2),
                  pl.BlockSpec((ts, d), row),
                  pl.BlockSpec((a.shape[0], ts, LANES), lambda b, s: (0, b * n_s + s, 0)),
                  pl.BlockSpec(w_out.shape, const2)],
        out_specs=pl.BlockSpec((ts, d), row),
        out_shape=jax.ShapeDtypeStruct((m, d), F32),
        scratch_shapes=[pltpu.VMEM((n_lt, CONV_HALO + ts, LANES), F32),
                        pltpu.VMEM((SUBLANES - 1, CONV_HALO + ts - SUBLANES, LANES), F32),
                        pltpu.VMEM((n_lt, ts, LANES), F32)],
        compiler_params=pltpu.CompilerParams(
            dimension_semantics=("arbitrary", "arbitrary"), vmem_limit_bytes=VMEM_LIMIT),
        name="conv_outproj",
    )(u, gate, w_tiles, b_tiles, ln_g, ln_b, x2d, a, w_out)


def kernel(x, norm_g, w_in, b_forget, q_norm_g, k_norm_g, conv_w, conv_b, conv_ln_g, conv_ln_b, w_out):
    batch, seq_len, d_model = x.shape
    depth = w_in.shape[0]
    n_heads = b_forget.shape[1]
    fox_width = n_heads * HEAD_DIM
    conv_width = conv_w.shape[2]
    assert fox_width == d_model and conv_width == d_model
    assert seq_len % ROW_TILE == 0 and seq_len % ATT_TILE == 0 and seq_len % CONV_TILE == 0
    assert n_heads * DECAY_LANES <= LANES and CONV_HALO >= CONV_KERNEL - 1

    o_f = 3 * fox_width
    o_gf = o_f + n_heads
    x2d = x.reshape(batch * seq_len, d_model)
    for l in range(depth):
        w = w_in[l].astype(BF16)
        w_main = jnp.concatenate([w[:, :o_f], w[:, o_gf:]], axis=1)
        w_f = jnp.pad(w[:, o_f:o_gf], ((0, 0), (0, LANES - n_heads)))
        b_f = jnp.pad(b_forget[l], (0, LANES - n_heads)).reshape(1, LANES)
        q, k, v, sgf, u, sgc, kdec, qdec = _inproj(
            x2d, norm_g[l].reshape(1, d_model), w_main, w_f, b_f,
            q_norm_g[l].reshape(1, fox_width), k_norm_g[l].reshape(1, fox_width), n_heads, seq_len)
        a = _attention(q, k, v, kdec, qdec, sgf, batch, seq_len)
        x2d = _conv_outproj(u, sgc, conv_w[l], conv_b[l],
                            conv_ln_g[l].reshape(1, conv_width), conv_ln_b[l].reshape(1, conv_width),
                            x2d, a, w_out[l].astype(BF16), batch, seq_len)
    return x2d.reshape(batch, seq_len, d_model)
```

```python
import functools
import math

import jax
import jax.numpy as jnp
import numpy as np
from jax import lax
from jax.experimental import pallas as pl
from jax.experimental.pallas import tpu as pltpu

F32 = jnp.float32
BF16 = jnp.bfloat16

HEAD_DIM = 64
CONV_KERNEL = 31
EPS = 1e-6
NEG_INF = -1e30
LOG2E = math.log2(math.e)

LANES = 128
SUBLANES = 8
BF16_ROWS = 16
HEADS_PER_STEP = LANES // HEAD_DIM
DECAY_LANES = 8
VMEM_LIMIT = 56 * 1024 * 1024

ROW_TILE = 512
ATT_TILE = 256
ATT_PIPE = 2
V_ROWS = HEAD_DIM + BF16_ROWS
CONV_HALO = 32
CONV_CHUNK = 128


def _split3(x):
    hi = x.astype(BF16)
    r1 = x - hi.astype(F32)
    mid = r1.astype(BF16)
    lo = (r1 - mid.astype(F32)).astype(BF16)
    return hi, mid, lo


def _sigmoid(x):
    return 0.5 * jnp.tanh(0.5 * x) + 0.5


def _silu(x):
    h = 0.5 * x
    return h + h * jnp.tanh(h)


def _head_norm(xf, gain):
    lo_half = lax.broadcasted_iota(jnp.int32, (1, LANES), 1) < HEAD_DIM
    sq = xf * xf
    s_lo = jnp.sum(jnp.where(lo_half, sq, 0.0), axis=-1, keepdims=True)
    s_hi = jnp.sum(jnp.where(lo_half, 0.0, sq), axis=-1, keepdims=True)
    mean_sq = jnp.where(lo_half, s_lo, s_hi) * (1.0 / HEAD_DIM)
    return xf * lax.rsqrt(mean_sq + EPS) * gain


def _inproj_kernel(x_ref, g_ref, w_ref, wf_ref, bf_ref, tri_ref, gq_ref, gk_ref, place_ref,
                   cw_ref, cb_ref, lng_ref, lnb_ref,
                   q_ref, k_ref, v_ref, sgf_ref, yu_ref, kdec_ref, qdec_ref,
                   carry_ref, ext_ref, sh_ref, y_ref, *, tiles_per_seq, width):
    i = pl.program_id(0)
    n_lt = width // LANES
    tm = x_ref.shape[0]
    first_tile = i % tiles_per_seq == 0

    @pl.when(first_tile)
    def _():
        carry_ref[...] = jnp.zeros_like(carry_ref)
        ext_ref[:, 0:CONV_HALO, :] = jnp.zeros((n_lt, CONV_HALO, LANES), F32)

    @pl.when(jnp.logical_not(first_tile))
    def _():
        ext_ref[:, 0:CONV_HALO, :] = ext_ref[:, tm:tm + CONV_HALO, :]

    x = x_ref[...]
    inv_rms = lax.rsqrt(jnp.mean(x * x, axis=-1, keepdims=True) + EPS)
    h = (x * g_ref[...]).astype(BF16)

    def mm(sec):
        return inv_rms * jnp.dot(h, w_ref[:, sec * width:(sec + 1) * width], preferred_element_type=F32)

    def lane_tile(a, p):
        return a[:, p * LANES:(p + 1) * LANES]

    def conv_lane_tile(lt):
        shifted = sh_ref.at[lt % 2]
        sh_rows = sh_ref.shape[2]
        for b in range(1, SUBLANES):
            shifted[b - 1] = ext_ref[lt, b:b + sh_rows, :]
        first = CONV_HALO - (CONV_KERNEL - 1)
        for r0 in range(0, tm, CONV_CHUNK):
            acc = jnp.zeros((CONV_CHUNK, LANES), F32) + cb_ref[lt]
            for j in range(CONV_KERNEL):
                a8, b = divmod(first + j, SUBLANES)
                rows = slice(r0 + a8 * SUBLANES, r0 + a8 * SUBLANES + CONV_CHUNK)
                tap = ext_ref[lt, rows, :] if b == 0 else shifted[b - 1, rows, :]
                acc = acc + tap * cw_ref[lt, j:j + 1, :]
            y_ref[lt, r0:r0 + CONV_CHUNK, :] = acc

    zf = inv_rms * jnp.dot(h, wf_ref[...], preferred_element_type=F32) + bf_ref[...]

    u = mm(4) * _sigmoid(mm(5))
    for lt in range(n_lt):
        ext_ref[lt, CONV_HALO:CONV_HALO + tm, :] = lane_tile(u, lt)

    q = mm(0)
    for p in range(n_lt):
        qn = _head_norm(lane_tile(q, p), gq_ref[:, p * LANES:(p + 1) * LANES])
        q_ref[p] = (qn * (LOG2E * HEAD_DIM ** -0.5)).astype(BF16)
    conv_lane_tile(0)
    conv_lane_tile(1)

    log_f = jnp.minimum(zf, 0.0) - jnp.log1p(jnp.exp(-jnp.abs(zf)))
    pieces = jnp.concatenate(_split3(log_f), axis=1)
    cs = jnp.dot(tri_ref[...], pieces, preferred_element_type=F32)
    c_full = cs[:, :LANES] + cs[:, LANES:2 * LANES] + cs[:, 2 * LANES:] + carry_ref[...]
    carry_ref[...] = c_full[c_full.shape[0] - 1:, :]
    c_pieces = jnp.concatenate(_split3(c_full * LOG2E), axis=1)

    k = mm(1)
    for p in range(n_lt):
        k_ref[p] = _head_norm(lane_tile(k, p), gk_ref[:, p * LANES:(p + 1) * LANES]).astype(BF16)
    conv_lane_tile(2)
    conv_lane_tile(3)

    slot = lax.broadcasted_iota(jnp.int32, (1, LANES), 1) & (DECAY_LANES - 1)
    dec = jnp.dot(c_pieces, place_ref[...], preferred_element_type=F32)
    kdec_ref[...] = (dec[:, :LANES] + jnp.where(slot < 3, 1.0, 0.0)).astype(BF16)
    qdec_ref[...] = (dec[:, LANES:] + jnp.where((slot >= 3) & (slot < 6), 1.0, 0.0)).astype(BF16)

    v = mm(2)
    for p in range(n_lt):
        v_ref[p] = lane_tile(v, p).astype(BF16)
    conv_lane_tile(4)
    conv_lane_tile(5)
    sgf = _silu(mm(3))
    for p in range(n_lt):
        sgf_ref[p] = lane_tile(sgf, p).astype(BF16)
    conv_lane_tile(6)
    conv_lane_tile(7)
    assert n_lt == 8

    gate = _silu(mm(6))
    y = jnp.concatenate([y_ref[lt] for lt in range(n_lt)], axis=1)
    mu = jnp.mean(y, axis=-1, keepdims=True)
    yc = y - mu
    var = jnp.mean(yc * yc, axis=-1, keepdims=True)
    z = yc * lax.rsqrt(var + EPS) * lng_ref[...] + lnb_ref[...]
    yu_ref[...] = (_silu(z) * gate).astype(BF16)


def _decay_placement(n_heads):
    pk = np.zeros((3 * LANES, LANES), np.float32)
    pq = np.zeros((3 * LANES, LANES), np.float32)
    for head in range(n_heads):
        for piece in range(3):
            pq[piece * LANES + head, DECAY_LANES * head + piece] = 1.0
            pk[piece * LANES + head, DECAY_LANES * head + 3 + piece] = -1.0
    return jnp.asarray(np.concatenate([pk, pq], axis=1), BF16)


def _inproj(x2d, norm_g, w_main, w_f, b_f, gq, gk, conv_w, conv_b, ln_g, ln_b, n_heads, seq_len):
    m, d = x2d.shape
    tm = ROW_TILE
    width = d
    n_lt = width // LANES
    tri = (lax.broadcasted_iota(jnp.int32, (tm, tm), 0)
           >= lax.broadcasted_iota(jnp.int32, (tm, tm), 1)).astype(BF16)
    place = _decay_placement(n_heads)
    w_tiles = conv_w.reshape(CONV_KERNEL, n_lt, LANES).transpose(1, 0, 2)
    b_tiles = conv_b.reshape(n_lt, 1, LANES)
    const3 = lambda i: (0, 0, 0)
    row = lambda i: (i, 0)
    const = lambda i: (0, 0)
    tiled = pl.BlockSpec((n_lt, tm, LANES), lambda i: (0, i, 0))
    out_tiled = jax.ShapeDtypeStruct((n_lt, m, LANES), BF16)
    out_bf = jax.ShapeDtypeStruct((m, width), BF16)
    out_dec = jax.ShapeDtypeStruct((m, LANES), BF16)
    return pl.pallas_call(
        functools.partial(_inproj_kernel, tiles_per_seq=seq_len // tm, width=width),
        grid=(m // tm,),
        in_specs=[
            pl.BlockSpec((tm, d), row),
            pl.BlockSpec((1, d), const),
            pl.BlockSpec(w_main.shape, const, pipeline_mode=pl.Buffered(1)),
            pl.BlockSpec(w_f.shape, const),
            pl.BlockSpec((1, LANES), const),
            pl.BlockSpec((tm, tm), const),
            pl.BlockSpec((1, width), const),
            pl.BlockSpec((1, width), const),
            pl.BlockSpec(place.shape, const),
            pl.BlockSpec(w_tiles.shape, const3),
            pl.BlockSpec(b_tiles.shape, const3),
            pl.BlockSpec((1, width), const),
            pl.BlockSpec((1, width), const),
        ],
        out_specs=[tiled] * 4 + [pl.BlockSpec((tm, width), row)] + [pl.BlockSpec((tm, LANES), row)] * 2,
        out_shape=[out_tiled] * 4 + [out_bf] + [out_dec] * 2,
        scratch_shapes=[pltpu.VMEM((1, LANES), F32),
                        pltpu.VMEM((n_lt, CONV_HALO + tm, LANES), F32),
                        pltpu.VMEM((2, SUBLANES - 1, CONV_HALO + tm - SUBLANES, LANES), F32),
                        pltpu.VMEM((n_lt, tm, LANES), F32)],
        compiler_params=pltpu.CompilerParams(
            dimension_semantics=("arbitrary",), vmem_limit_bytes=VMEM_LIMIT),
        name="inproj",
    )(x2d, norm_g, w_main, w_f, b_f, tri, gq, gk, place, w_tiles, b_tiles, ln_g, ln_b)


def _attn_kernel(q_ref, k_ref, v_ref, kdec_ref, qdec_ref, gate_ref, o_ref,
                 qaugt_ref, vt_ref, s_ref, bias_ref, acc_ref, m_ref):
    t = ATT_TILE
    seq = q_ref.shape[1]
    n_tiles = seq // t
    pair = pl.program_id(1)
    sub = lax.broadcasted_iota(jnp.int32, (LANES, 1), 0)

    def prepare(r):
        rows = slice(r * t, (r + 1) * t)
        q_t = q_ref[0, rows, :].astype(F32).T
        d_t = qdec_ref[rows, :].astype(F32).T
        v_t = v_ref[0, rows, :].astype(F32).T
        for h in range(HEADS_PER_STEP):
            in_head = (sub >= h * HEAD_DIM) & (sub < (h + 1) * HEAD_DIM)
            qaugt_ref[h, r, 0:LANES, :] = jnp.where(in_head, q_t, 0.0).astype(BF16)
            first_lane = (pair * HEADS_PER_STEP + h) * DECAY_LANES
            own_decay = (sub >= first_lane) & (sub < first_lane + DECAY_LANES)
            qaugt_ref[h, r, LANES:2 * LANES, :] = jnp.where(own_decay, d_t, 0.0).astype(BF16)
            vt_ref[r, h * V_ROWS:h * V_ROWS + HEAD_DIM, :] = v_t[h * HEAD_DIM:(h + 1) * HEAD_DIM, :].astype(BF16)
            vt_ref[r, h * V_ROWS + HEAD_DIM:(h + 1) * V_ROWS, :] = jnp.ones((V_ROWS - HEAD_DIM, t), BF16)

    k_idx = lax.broadcasted_iota(jnp.int32, (t, t), 0)
    q_idx = lax.broadcasted_iota(jnp.int32, (t, t), 1)
    bias_ref[...] = jnp.where(k_idx <= q_idx, jnp.inf, NEG_INF)
    acc_ref[...] = jnp.zeros_like(acc_ref)
    m_ref[...] = jnp.full(m_ref.shape, NEG_INF, F32)

    def qk(i, j):
        kaug = jnp.concatenate([k_ref[0, j * t:(j + 1) * t, :], kdec_ref[j * t:(j + 1) * t, :]], axis=1)
        return [jnp.dot(kaug, qaugt_ref[h, i], preferred_element_type=F32) for h in range(HEADS_PER_STEP)]

    def store_scores(slot, i, j, s):
        blk_max = []
        for h in range(HEADS_PER_STEP):
            s_h = jnp.minimum(s[h], bias_ref[...]) if i == j else s[h]
            blk_max.append(jnp.max(s_h, axis=0, keepdims=True))
            s_ref[slot, h] = s_h
        return blk_max

    def softmax_pv(slot, i, j, blk_max):
        for h in range(HEADS_PER_STEP):
            m_old = m_ref[i, h:h + 1, :]
            m_new = jnp.maximum(m_old, blk_max[h])
            p = jnp.exp2(s_ref[slot, h] - m_new)
            alpha = jnp.exp2(m_old - m_new)
            hrows = slice(h * V_ROWS, (h + 1) * V_ROWS)
            pv = jnp.dot(vt_ref[j, hrows, :], p.astype(BF16), preferred_element_type=F32)
            acc_ref[i, hrows, :] = alpha * acc_ref[i, hrows, :] + pv
            m_ref[i, h:h + 1, :] = m_new

    def finalize(i):
        parts = []
        for h in range(HEADS_PER_STEP):
            row_sum = acc_ref[i, h * V_ROWS + HEAD_DIM:h * V_ROWS + HEAD_DIM + 1, :]
            parts.append(acc_ref[i, h * V_ROWS:h * V_ROWS + HEAD_DIM, :] * (1.0 / row_sum))
        rows = slice(i * t, (i + 1) * t)
        out_t = jnp.concatenate(parts, axis=0)
        o_ref[0, rows, :] = (out_t.T * gate_ref[0, rows, :].astype(F32)).astype(BF16)

    blocks = [(i, j) for i in range(n_tiles) for j in range(i + 1)]
    prepared = 0

    def issue(n):
        nonlocal prepared
        while prepared <= min(blocks[n][0] + 1, n_tiles - 1):
            prepare(prepared)
            prepared += 1
        return qk(*blocks[n])

    maxes = [store_scores(n, *blocks[n], issue(n)) for n in range(ATT_PIPE)]
    for n, (i, j) in enumerate(blocks):
        slot = n % ATT_PIPE
        ahead = n + ATT_PIPE
        if ahead < len(blocks):
            s_next = issue(ahead)
        softmax_pv(slot, i, j, maxes[slot])
        if ahead < len(blocks):
            maxes[slot] = store_scores(slot, *blocks[ahead], s_next)
        if i == j:
            finalize(i)


def _attention(q, k, v, kdec, qdec, gate, batch, seq_len):
    n_pairs, m, _ = q.shape
    t = ATT_TILE
    slab = pl.BlockSpec((1, seq_len, LANES), lambda b, p: (p, b, 0))
    shared = pl.BlockSpec((seq_len, LANES), lambda b, p: (b, 0))
    return pl.pallas_call(
        _attn_kernel,
        grid=(batch, n_pairs),
        in_specs=[slab, slab, slab, shared, shared, slab],
        out_specs=slab,
        out_shape=jax.ShapeDtypeStruct((n_pairs, m, LANES), BF16),
        scratch_shapes=[
            pltpu.VMEM((HEADS_PER_STEP, seq_len // t, 2 * LANES, t), BF16),
            pltpu.VMEM((seq_len // t, HEADS_PER_STEP * V_ROWS, t), BF16),
            pltpu.VMEM((ATT_PIPE, HEADS_PER_STEP, t, t), F32),
            pltpu.VMEM((t, t), F32),
            pltpu.VMEM((seq_len // t, HEADS_PER_STEP * V_ROWS, t), F32),
            pltpu.VMEM((seq_len // t, HEADS_PER_STEP, t), F32),
        ],
        compiler_params=pltpu.CompilerParams(
            dimension_semantics=("arbitrary", "arbitrary"), vmem_limit_bytes=VMEM_LIMIT),
        name="fox_attention",
    )(q, k, v, kdec, qdec, gate)


def _outproj_kernel(x_ref, a_ref, u_ref, w_ref, o_ref):
    a = jnp.concatenate([a_ref[p] for p in range(a_ref.shape[0])], axis=1)
    half = a.shape[1]
    o_ref[...] = (x_ref[...]
                  + jnp.dot(a, w_ref[0:half, :], preferred_element_type=F32)
                  + jnp.dot(u_ref[...], w_ref[half:, :], preferred_element_type=F32))


def _outproj(x2d, a, u, w_out):
    m, d = x2d.shape
    tm = ROW_TILE
    row = lambda i: (i, 0)
    return pl.pallas_call(
        _outproj_kernel,
        grid=(m // tm,),
        in_specs=[pl.BlockSpec((tm, d), row), pl.BlockSpec((a.shape[0], tm, LANES), lambda i: (0, i, 0)),
                  pl.BlockSpec((tm, u.shape[1]), row),
                  pl.BlockSpec(w_out.shape, lambda i: (0, 0))],
        out_specs=pl.BlockSpec((tm, d), row),
        out_shape=jax.ShapeDtypeStruct((m, d), F32),
        compiler_params=pltpu.CompilerParams(
            dimension_semantics=("arbitrary",), vmem_limit_bytes=VMEM_LIMIT),
        name="outproj",
    )(x2d, a, u, w_out)


def kernel(x, norm_g, w_in, b_forget, q_norm_g, k_norm_g, conv_w, conv_b, conv_ln_g, conv_ln_b, w_out):
    batch, seq_len, d_model = x.shape
    depth = w_in.shape[0]
    n_heads = b_forget.shape[1]
    fox_width = n_heads * HEAD_DIM
    conv_width = conv_w.shape[2]
    assert fox_width == d_model and conv_width == d_model
    assert seq_len % ROW_TILE == 0 and seq_len % ATT_TILE == 0 and ROW_TILE % CONV_CHUNK == 0
    assert n_heads * DECAY_LANES <= LANES and CONV_HALO >= CONV_KERNEL - 1

    o_f = 3 * fox_width
    o_gf = o_f + n_heads
    x2d = x.reshape(batch * seq_len, d_model)
    for l in range(depth):
        w = w_in[l].astype(BF16)
        w_main = jnp.concatenate([w[:, :o_f], w[:, o_gf:]], axis=1)
        w_f = jnp.pad(w[:, o_f:o_gf], ((0, 0), (0, LANES - n_heads)))
        b_f = jnp.pad(b_forget[l], (0, LANES - n_heads)).reshape(1, LANES)
        q, k, v, sgf, yu, kdec, qdec = _inproj(
            x2d, norm_g[l].reshape(1, d_model), w_main, w_f, b_f,
            q_norm_g[l].reshape(1, fox_width), k_norm_g[l].reshape(1, fox_width),
            conv_w[l], conv_b[l], conv_ln_g[l].reshape(1, conv_width), conv_ln_b[l].reshape(1, conv_width),
            n_heads, seq_len)
        a = _attention(q, k, v, kdec, qdec, sgf, batch, seq_len)
        x2d = _outproj(x2d, a, yu, w_out[l].astype(BF16))
    return x2d.reshape(batch, seq_len, d_model)
```

```python
import functools
import math

import jax
import jax.numpy as jnp
import numpy as np
from jax import lax
from jax.experimental import pallas as pl
from jax.experimental.pallas import tpu as pltpu

F32 = jnp.float32
BF16 = jnp.bfloat16

HEAD_DIM = 64
CONV_KERNEL = 31
EPS = 1e-6
NEG_INF = -1e30
LOG2E = math.log2(math.e)

LANES = 128
SUBLANES = 8
BF16_ROWS = 16
HEADS_PER_STEP = LANES // HEAD_DIM
DECAY_LANES = 8
VMEM_LIMIT = 56 * 1024 * 1024

ROW_TILE = 512
OUT_TILE = 1024
ATT_TILE = 256
ATT_PIPE = 2
V_ROWS = HEAD_DIM + BF16_ROWS
CONV_HALO = 32
CONV_CHUNK = 128


def _split3(x):
    hi = x.astype(BF16)
    r1 = x - hi.astype(F32)
    mid = r1.astype(BF16)
    lo = (r1 - mid.astype(F32)).astype(BF16)
    return hi, mid, lo


def _sigmoid(x):
    return 0.5 * jnp.tanh(0.5 * x) + 0.5


def _silu(x):
    h = 0.5 * x
    return h + h * jnp.tanh(h)


def _head_norm(xf, gain):
    lo_half = lax.broadcasted_iota(jnp.int32, (1, LANES), 1) < HEAD_DIM
    sq = xf * xf
    s_lo = jnp.sum(jnp.where(lo_half, sq, 0.0), axis=-1, keepdims=True)
    s_hi = jnp.sum(jnp.where(lo_half, 0.0, sq), axis=-1, keepdims=True)
    mean_sq = jnp.where(lo_half, s_lo, s_hi) * (1.0 / HEAD_DIM)
    return xf * lax.rsqrt(mean_sq + EPS) * gain


def _inproj_kernel(x_ref, g_ref, w_ref, wf_ref, bf_ref, tri_ref, gq_ref, gk_ref, place_ref,
                   cw_ref, cb_ref, lng_ref, lnb_ref,
                   q_ref, k_ref, v_ref, sgf_ref, yu_ref, kdec_ref, qdec_ref,
                   carry_ref, ext_ref, sh_ref, y_ref, *, tiles_per_seq, width):
    i = pl.program_id(0)
    n_lt = width // LANES
    tm = x_ref.shape[0]
    first_tile = i % tiles_per_seq == 0

    @pl.when(first_tile)
    def _():
        carry_ref[...] = jnp.zeros_like(carry_ref)
        ext_ref[:, 0:CONV_HALO, :] = jnp.zeros((n_lt, CONV_HALO, LANES), F32)

    @pl.when(jnp.logical_not(first_tile))
    def _():
        ext_ref[:, 0:CONV_HALO, :] = ext_ref[:, tm:tm + CONV_HALO, :]

    x = x_ref[...]
    inv_rms = lax.rsqrt(jnp.mean(x * x, axis=-1, keepdims=True) + EPS)
    h = (x * g_ref[...]).astype(BF16)

    def mm(sec):
        return inv_rms * jnp.dot(h, w_ref[:, sec * width:(sec + 1) * width], preferred_element_type=F32)

    def lane_tile(a, p):
        return a[:, p * LANES:(p + 1) * LANES]

    def conv_lane_tile(lt):
        shifted = sh_ref.at[lt % 2]
        sh_rows = sh_ref.shape[2]
        for b in range(1, SUBLANES):
            shifted[b - 1] = ext_ref[lt, b:b + sh_rows, :]
        first = CONV_HALO - (CONV_KERNEL - 1)
        for r0 in range(0, tm, CONV_CHUNK):
            acc = jnp.zeros((CONV_CHUNK, LANES), F32) + cb_ref[lt]
            for j in range(CONV_KERNEL):
                a8, b = divmod(first + j, SUBLANES)
                rows = slice(r0 + a8 * SUBLANES, r0 + a8 * SUBLANES + CONV_CHUNK)
                tap = ext_ref[lt, rows, :] if b == 0 else shifted[b - 1, rows, :]
                acc = acc + tap * cw_ref[lt, j:j + 1, :]
            y_ref[lt, r0:r0 + CONV_CHUNK, :] = acc

    zf = inv_rms * jnp.dot(h, wf_ref[...], preferred_element_type=F32) + bf_ref[...]

    u = mm(4) * _sigmoid(mm(5))
    for lt in range(n_lt):
        ext_ref[lt, CONV_HALO:CONV_HALO + tm, :] = lane_tile(u, lt)

    q = mm(0)
    for p in range(n_lt):
        qn = _head_norm(lane_tile(q, p), gq_ref[:, p * LANES:(p + 1) * LANES])
        q_ref[p] = (qn * (LOG2E * HEAD_DIM ** -0.5)).astype(BF16)
    conv_lane_tile(0)
    conv_lane_tile(1)

    log_f = jnp.minimum(zf, 0.0) - jnp.log1p(jnp.exp(-jnp.abs(zf)))
    pieces = jnp.concatenate(_split3(log_f), axis=1)
    cs = jnp.dot(tri_ref[...], pieces, preferred_element_type=F32)
    c_full = cs[:, :LANES] + cs[:, LANES:2 * LANES] + cs[:, 2 * LANES:] + carry_ref[...]
    carry_ref[...] = c_full[c_full.shape[0] - 1:, :]
    c_pieces = jnp.concatenate(_split3(c_full * LOG2E), axis=1)

    k = mm(1)
    for p in range(n_lt):
        k_ref[p] = _head_norm(lane_tile(k, p), gk_ref[:, p * LANES:(p + 1) * LANES]).astype(BF16)
    conv_lane_tile(2)
    conv_lane_tile(3)

    slot = lax.broadcasted_iota(jnp.int32, (1, LANES), 1) & (DECAY_LANES - 1)
    dec = jnp.dot(c_pieces, place_ref[...], preferred_element_type=F32)
    kdec_ref[...] = (dec[:, :LANES] + jnp.where(slot < 3, 1.0, 0.0)).astype(BF16)
    qdec_ref[...] = (dec[:, LANES:] + jnp.where((slot >= 3) & (slot < 6), 1.0, 0.0)).astype(BF16)

    v = mm(2)
    for p in range(n_lt):
        v_ref[p] = lane_tile(v, p).astype(BF16)
    conv_lane_tile(4)
    conv_lane_tile(5)
    sgf = _silu(mm(3))
    for p in range(n_lt):
        sgf_ref[p] = lane_tile(sgf, p).astype(BF16)
    conv_lane_tile(6)
    conv_lane_tile(7)
    assert n_lt == 8

    gate = _silu(mm(6))
    y = jnp.concatenate([y_ref[lt] for lt in range(n_lt)], axis=1)
    mu = jnp.mean(y, axis=-1, keepdims=True)
    yc = y - mu
    var = jnp.mean(yc * yc, axis=-1, keepdims=True)
    z = yc * lax.rsqrt(var + EPS) * lng_ref[...] + lnb_ref[...]
    yu_ref[...] = (_silu(z) * gate).astype(BF16)


def _decay_placement(n_heads):
    pk = np.zeros((3 * LANES, LANES), np.float32)
    pq = np.zeros((3 * LANES, LANES), np.float32)
    for head in range(n_heads):
        for piece in range(3):
            pq[piece * LANES + head, DECAY_LANES * head + piece] = 1.0
            pk[piece * LANES + head, DECAY_LANES * head + 3 + piece] = -1.0
    return jnp.asarray(np.concatenate([pk, pq], axis=1), BF16)


def _inproj(x2d, norm_g, w_main, w_f, b_f, gq, gk, conv_w, conv_b, ln_g, ln_b, n_heads, seq_len):
    m, d = x2d.shape
    tm = ROW_TILE
    width = d
    n_lt = width // LANES
    tri = (lax.broadcasted_iota(jnp.int32, (tm, tm), 0)
           >= lax.broadcasted_iota(jnp.int32, (tm, tm), 1)).astype(BF16)
    place = _decay_placement(n_heads)
    w_tiles = conv_w.reshape(CONV_KERNEL, n_lt, LANES).transpose(1, 0, 2)
    b_tiles = conv_b.reshape(n_lt, 1, LANES)
    const3 = lambda i: (0, 0, 0)
    row = lambda i: (i, 0)
    const = lambda i: (0, 0)
    tiled = pl.BlockSpec((n_lt, tm, LANES), lambda i: (0, i, 0))
    out_tiled = jax.ShapeDtypeStruct((n_lt, m, LANES), BF16)
    out_bf = jax.ShapeDtypeStruct((m, width), BF16)
    out_dec = jax.ShapeDtypeStruct((m, LANES), BF16)
    return pl.pallas_call(
        functools.partial(_inproj_kernel, tiles_per_seq=seq_len // tm, width=width),
        grid=(m // tm,),
        in_specs=[
            pl.BlockSpec((tm, d), row),
            pl.BlockSpec((1, d), const),
            pl.BlockSpec(w_main.shape, const, pipeline_mode=pl.Buffered(1)),
            pl.BlockSpec(w_f.shape, const),
            pl.BlockSpec((1, LANES), const),
            pl.BlockSpec((tm, tm), const),
            pl.BlockSpec((1, width), const),
            pl.BlockSpec((1, width), const),
            pl.BlockSpec(place.shape, const),
            pl.BlockSpec(w_tiles.shape, const3),
            pl.BlockSpec(b_tiles.shape, const3),
            pl.BlockSpec((1, width), const),
            pl.BlockSpec((1, width), const),
        ],
        out_specs=[tiled] * 4 + [pl.BlockSpec((tm, width), row)] + [pl.BlockSpec((tm, LANES), row)] * 2,
        out_shape=[out_tiled] * 4 + [out_bf] + [out_dec] * 2,
        scratch_shapes=[pltpu.VMEM((1, LANES), F32),
                        pltpu.VMEM((n_lt, CONV_HALO + tm, LANES), F32),
                        pltpu.VMEM((2, SUBLANES - 1, CONV_HALO + tm - SUBLANES, LANES), F32),
                        pltpu.VMEM((n_lt, tm, LANES), F32)],
        compiler_params=pltpu.CompilerParams(
            dimension_semantics=("arbitrary",), vmem_limit_bytes=VMEM_LIMIT),
        name="inproj",
    )(x2d, norm_g, w_main, w_f, b_f, tri, gq, gk, place, w_tiles, b_tiles, ln_g, ln_b)


def _attn_kernel(q_ref, k_ref, v_ref, kdec_ref, qdec_ref, gate_ref, o_ref,
                 qaugt_ref, vt_ref, s_ref, bias_ref, acc_ref, m_ref):
    t = ATT_TILE
    seq = q_ref.shape[1]
    n_tiles = seq // t
    pair = pl.program_id(1)
    sub = lax.broadcasted_iota(jnp.int32, (LANES, 1), 0)

    def prepare(r):
        rows = slice(r * t, (r + 1) * t)
        q_t = q_ref[0, rows, :].astype(F32).T
        d_t = qdec_ref[rows, :].astype(F32).T
        v_t = v_ref[0, rows, :].astype(F32).T
        for h in range(HEADS_PER_STEP):
            in_head = (sub >= h * HEAD_DIM) & (sub < (h + 1) * HEAD_DIM)
            qaugt_ref[h, r, 0:LANES, :] = jnp.where(in_head, q_t, 0.0).astype(BF16)
            first_lane = (pair * HEADS_PER_STEP + h) * DECAY_LANES
            own_decay = (sub >= first_lane) & (sub < first_lane + DECAY_LANES)
            qaugt_ref[h, r, LANES:2 * LANES, :] = jnp.where(own_decay, d_t, 0.0).astype(BF16)
            vt_ref[r, h * V_ROWS:h * V_ROWS + HEAD_DIM, :] = v_t[h * HEAD_DIM:(h + 1) * HEAD_DIM, :].astype(BF16)
            vt_ref[r, h * V_ROWS + HEAD_DIM:(h + 1) * V_ROWS, :] = jnp.ones((V_ROWS - HEAD_DIM, t), BF16)

    k_idx = lax.broadcasted_iota(jnp.int32, (t, t), 0)
    q_idx = lax.broadcasted_iota(jnp.int32, (t, t), 1)
    bias_ref[...] = jnp.where(k_idx <= q_idx, jnp.inf, NEG_INF)
    acc_ref[...] = jnp.zeros_like(acc_ref)
    m_ref[...] = jnp.full(m_ref.shape, NEG_INF, F32)

    def qk(i, j):
        kaug = jnp.concatenate([k_ref[0, j * t:(j + 1) * t, :], kdec_ref[j * t:(j + 1) * t, :]], axis=1)
        return [jnp.dot(kaug, qaugt_ref[h, i], preferred_element_type=F32) for h in range(HEADS_PER_STEP)]

    def store_scores(slot, i, j, s):
        blk_max = []
        for h in range(HEADS_PER_STEP):
            s_h = jnp.minimum(s[h], bias_ref[...]) if i == j else s[h]
            blk_max.append(jnp.max(s_h, axis=0, keepdims=True))
            s_ref[slot, h] = s_h
        return blk_max

    def softmax_pv(slot, i, j, blk_max):
        for h in range(HEADS_PER_STEP):
            m_old = m_ref[i, h:h + 1, :]
            m_new = jnp.maximum(m_old, blk_max[h])
            p = jnp.exp2(s_ref[slot, h] - m_new)
            alpha = jnp.exp2(m_old - m_new)
            hrows = slice(h * V_ROWS, (h + 1) * V_ROWS)
            pv = jnp.dot(vt_ref[j, hrows, :], p.astype(BF16), preferred_element_type=F32)
            acc_ref[i, hrows, :] = alpha * acc_ref[i, hrows, :] + pv
            m_ref[i, h:h + 1, :] = m_new

    def finalize(i):
        parts = []
        for h in range(HEADS_PER_STEP):
            row_sum = acc_ref[i, h * V_ROWS + HEAD_DIM:h * V_ROWS + HEAD_DIM + 1, :]
            parts.append(acc_ref[i, h * V_ROWS:h * V_ROWS + HEAD_DIM, :] * (1.0 / row_sum))
        rows = slice(i * t, (i + 1) * t)
        out_t = jnp.concatenate(parts, axis=0)
        o_ref[0, rows, :] = (out_t.T * gate_ref[0, rows, :].astype(F32)).astype(BF16)

    blocks = [(i, j) for i in range(n_tiles) for j in range(i + 1)]
    prepared = 0

    def issue(n):
        nonlocal prepared
        while prepared <= min(blocks[n][0] + 1, n_tiles - 1):
            prepare(prepared)
            prepared += 1
        return qk(*blocks[n])

    maxes = [store_scores(n, *blocks[n], issue(n)) for n in range(ATT_PIPE)]
    for n, (i, j) in enumerate(blocks):
        slot = n % ATT_PIPE
        ahead = n + ATT_PIPE
        if ahead < len(blocks):
            s_next = issue(ahead)
        softmax_pv(slot, i, j, maxes[slot])
        if ahead < len(blocks):
            maxes[slot] = store_scores(slot, *blocks[ahead], s_next)
        if i == j:
            finalize(i)


def _attention(q, k, v, kdec, qdec, gate, batch, seq_len):
    n_pairs, m, _ = q.shape
    t = ATT_TILE
    slab = pl.BlockSpec((1, seq_len, LANES), lambda b, p: (p, b, 0))
    shared = pl.BlockSpec((seq_len, LANES), lambda b, p: (b, 0))
    return pl.pallas_call(
        _attn_kernel,
        grid=(batch, n_pairs),
        in_specs=[slab, slab, slab, shared, shared, slab],
        out_specs=slab,
        out_shape=jax.ShapeDtypeStruct((n_pairs, m, LANES), BF16),
        scratch_shapes=[
            pltpu.VMEM((HEADS_PER_STEP, seq_len // t, 2 * LANES, t), BF16),
            pltpu.VMEM((seq_len // t, HEADS_PER_STEP * V_ROWS, t), BF16),
            pltpu.VMEM((ATT_PIPE, HEADS_PER_STEP, t, t), F32),
            pltpu.VMEM((t, t), F32),
            pltpu.VMEM((seq_len // t, HEADS_PER_STEP * V_ROWS, t), F32),
            pltpu.VMEM((seq_len // t, HEADS_PER_STEP, t), F32),
        ],
        compiler_params=pltpu.CompilerParams(
            dimension_semantics=("arbitrary", "arbitrary"), vmem_limit_bytes=VMEM_LIMIT),
        name="fox_attention",
    )(q, k, v, kdec, qdec, gate)


def _outproj_kernel(x_ref, a_ref, u_ref, w_ref, o_ref):
    a = jnp.concatenate([a_ref[p] for p in range(a_ref.shape[0])], axis=1)
    half = a.shape[1]
    o_ref[...] = (x_ref[...]
                  + jnp.dot(a, w_ref[0:half, :], preferred_element_type=F32)
                  + jnp.dot(u_ref[...], w_ref[half:, :], preferred_element_type=F32))


def _outproj(x2d, a, u, w_out):
    m, d = x2d.shape
    tm = OUT_TILE
    assert m % tm == 0
    row = lambda i: (i, 0)
    return pl.pallas_call(
        _outproj_kernel,
        grid=(m // tm,),
        in_specs=[pl.BlockSpec((tm, d), row), pl.BlockSpec((a.shape[0], tm, LANES), lambda i: (0, i, 0)),
                  pl.BlockSpec((tm, u.shape[1]), row),
                  pl.BlockSpec(w_out.shape, lambda i: (0, 0))],
        out_specs=pl.BlockSpec((tm, d), row),
        out_shape=jax.ShapeDtypeStruct((m, d), F32),
        compiler_params=pltpu.CompilerParams(
            dimension_semantics=("arbitrary",), vmem_limit_bytes=VMEM_LIMIT),
        name="outproj",
    )(x2d, a, u, w_out)


def kernel(x, norm_g, w_in, b_forget, q_norm_g, k_norm_g, conv_w, conv_b, conv_ln_g, conv_ln_b, w_out):
    batch, seq_len, d_model = x.shape
    depth = w_in.shape[0]
    n_heads = b_forget.shape[1]
    fox_width = n_heads * HEAD_DIM
    conv_width = conv_w.shape[2]
    assert fox_width == d_model and conv_width == d_model
    assert seq_len % ROW_TILE == 0 and seq_len % ATT_TILE == 0 and ROW_TILE % CONV_CHUNK == 0
    assert n_heads * DECAY_LANES <= LANES and CONV_HALO >= CONV_KERNEL - 1

    o_f = 3 * fox_width
    o_gf = o_f + n_heads
    x2d = x.reshape(batch * seq_len, d_model)
    for l in range(depth):
        w = w_in[l].astype(BF16)
        w_main = jnp.concatenate([w[:, :o_f], w[:, o_gf:]], axis=1)
        w_f = jnp.pad(w[:, o_f:o_gf], ((0, 0), (0, LANES - n_heads)))
        b_f = jnp.pad(b_forget[l], (0, LANES - n_heads)).reshape(1, LANES)
        q, k, v, sgf, yu, kdec, qdec = _inproj(
            x2d, norm_g[l].reshape(1, d_model), w_main, w_f, b_f,
            q_norm_g[l].reshape(1, fox_width), k_norm_g[l].reshape(1, fox_width),
            conv_w[l], conv_b[l], conv_ln_g[l].reshape(1, conv_width), conv_ln_b[l].reshape(1, conv_width),
            n_heads, seq_len)
        a = _attention(q, k, v, kdec, qdec, sgf, batch, seq_len)
        x2d = _outproj(x2d, a, yu, w_out[l].astype(BF16))
    return x2d.reshape(batch, seq_len, d_model)
```
